```python
import jax
import jax.numpy as jnp
from jax import lax
import numpy as np

D_MODEL = 1024
BATCH = 32
SEQ = 256
DEPTH = 2
DEC_BATCH = 2
DEC_SEQ = 4096
PAST_LEN = 256

GRID_W = 64
D_LRU = 1024
LRU_BLOCKS = 8
LRU_BLOCK = D_LRU // LRU_BLOCKS
LRU_CONV_W = 4
LRU_CONV_PAD_L = 2
LRU_C = 8.0
HEAD_DIM = 64
HEADS_B = 8
KV_HEADS_B = 2
HEADS_C = 8
KV_HEADS_C = 2
D_QB = HEADS_B * HEAD_DIM
D_KVB = KV_HEADS_B * HEAD_DIM
D_QC = HEADS_C * HEAD_DIM
D_KVC = KV_HEADS_C * HEAD_DIM
WINDOW = 128
Q_BLOCK = 128
D_FF = 2816
FFN_CONV_W = 3
FFN_CONV_PAD_L = 1
ROPE_THETA = 10000.0
NORM_EPS = 1e-6
NEG_INF = -1e30
IN_SIZES = (D_LRU, D_LRU, D_QB, D_KVB, D_KVB, D_QC, D_KVC, D_KVC, D_MODEL, D_MODEL, D_MODEL)
D_IN = sum(IN_SIZES)

kernel_name = "hybrid_diffusion_prefix_step"


def rmsnorm(x, g):
    xf = x.astype(jnp.float32)
    y = xf * lax.rsqrt(jnp.mean(xf * xf, axis=-1, keepdims=True) + NORM_EPS)
    return (y * g.astype(jnp.float32)).astype(x.dtype)


def adaln_params(cond, w_mod, b_mod):
    m = jax.nn.silu(cond) @ w_mod + b_mod
    return [t[:, None, :] for t in jnp.split(m, 6, axis=-1)]


def modulate(x, g, shift, scale):
    return rmsnorm(x, g) * (1 + scale) + shift


def split_in(z):
    idx = np.cumsum(np.array(IN_SIZES))[:-1].tolist()
    return jnp.split(z, idx, axis=-1)


def dwconv(x, w, b, pad_left):
    k_w = w.shape[0]
    s = x.shape[1]
    xp = jnp.pad(x, ((0, 0), (pad_left, k_w - 1 - pad_left), (0, 0)))
    y = xp[:, 0:s] * w[0] + b
    for k in range(1, k_w):
        y = y + xp[:, k:k + s] * w[k]
    return y


def _rotate(x, pos):
    nf = x.shape[-1] // 2
    inv = ROPE_THETA ** (-jnp.arange(nf, dtype=jnp.float32) / nf)
    ang = pos.astype(jnp.float32)[:, None] * inv[None, :]
    cos = jnp.cos(ang)[None, :, None, :]
    sin = jnp.sin(ang)[None, :, None, :]
    x1, x2 = x[..., :nf], x[..., nf:]
    return jnp.concatenate([x1 * cos - x2 * sin, x1 * sin + x2 * cos], axis=-1)


def axial_rope(x):
    s = x.shape[1]
    rows = s // GRID_W
    row, col = jnp.meshgrid(jnp.arange(rows), jnp.arange(GRID_W), indexing="ij")
    xf = x.astype(jnp.float32)
    half = x.shape[-1] // 2
    y = jnp.concatenate([_rotate(xf[..., :half], row.reshape(-1)),
                         _rotate(xf[..., half:], col.reshape(-1))], axis=-1)
    return y.astype(x.dtype)


def rglru(x, wa, ba, wx, bx, lam, h0, reverse):
    b, s, c = x.shape
    xb = x.reshape(b, s, LRU_BLOCKS, LRU_BLOCK)
    r = jax.nn.sigmoid(jnp.einsum("bsnc,ncd->bsnd", xb, wa).reshape(b, s, c) + ba)
    i = jax.nn.sigmoid(jnp.einsum("bsnc,ncd->bsnd", xb, wx).reshape(b, s, c) + bx)
    log_a = -LRU_C * r.astype(jnp.float32) * jax.nn.softplus(-lam.astype(jnp.float32))
    a = jnp.exp(log_a)
    u = jnp.sqrt(-jnp.expm1(2.0 * log_a)) * (i * x).astype(jnp.float32)

    def step(h, au):
        a_t, u_t = au
        h = a_t * h + u_t
        return h, h

    h_last, hs = lax.scan(step, h0.astype(jnp.float32),
                          (a.transpose(1, 0, 2), u.transpose(1, 0, 2)), reverse=reverse)
    return hs.transpose(1, 0, 2).astype(x.dtype), h_last


def gqa_blocked(q, k, v, sink=None):
    b, s, hq, hd = q.shape
    hkv = k.shape[2]
    g = hq // hkv
    nb = s // Q_BLOCK
    qb = q.reshape(b, nb, Q_BLOCK, hkv, g, hd).transpose(1, 0, 2, 3, 4, 5)
    scale = hd ** -0.5

    def block(qblk):
        sc = jnp.einsum("bqkgd,blkd->bkgql", qblk, k,
                        preferred_element_type=jnp.float32) * scale
        if sink is None:
            p = jax.nn.softmax(sc, axis=-1)
        else:
            sk = jnp.broadcast_to(sink.astype(jnp.float32).reshape(1, hkv, g, 1, 1),
                                  sc.shape[:-1] + (1,))
            p = jax.nn.softmax(jnp.concatenate([sc, sk], axis=-1), axis=-1)[..., :-1]
        return jnp.einsum("bkgql,blkd->bqkgd", p.astype(v.dtype), v)

    o = lax.map(block, qb)
    return o.transpose(1, 0, 2, 3, 4, 5).reshape(b, s, hq * hd)


def window_attn_latent(q, k, v, k_ctx, v_ctx, sink):
    b, s, hq, hd = q.shape
    hkv = k.shape[2]
    g = hq // hkv
    nb = s // WINDOW
    pad = ((0, 0), (WINDOW, WINDOW), (0, 0), (0, 0))
    kp = jnp.pad(k, pad).reshape(b, nb + 2, WINDOW, hkv, hd)
    vp = jnp.pad(v, pad).reshape(b, nb + 2, WINDOW, hkv, hd)
    k_band = jnp.concatenate([kp[:, :-2], kp[:, 1:-1], kp[:, 2:]], axis=2)
    v_band = jnp.concatenate([vp[:, :-2], vp[:, 1:-1], vp[:, 2:]], axis=2)
    qb = q.reshape(b, nb, WINDOW, hkv, g, hd)
    scale = hd ** -0.5
    s_lat = jnp.einsum("bnqkgd,bnlkd->bnkgql", qb, k_band,
                       preferred_element_type=jnp.float32) * scale
    blk = jnp.arange(nb)[:, None, None] * WINDOW
    q_abs = blk + jnp.arange(WINDOW)[None, :, None]
    k_abs = blk + jnp.arange(3 * WINDOW)[None, None, :] - WINDOW
    valid = (jnp.abs(q_abs - k_abs) <= WINDOW) & (k_abs >= 0) & (k_abs < s)
    s_lat = jnp.where(valid[None, :, None, None], s_lat, NEG_INF)
    s_ctx = jnp.einsum("bnqkgd,blkd->bnkgql", qb, k_ctx,
                       preferred_element_type=jnp.float32) * scale
    sk = jnp.broadcast_to(sink.astype(jnp.float32).reshape(1, 1, hkv, g, 1, 1),
                          s_ctx.shape[:-1] + (1,))
    p = jax.nn.softmax(jnp.concatenate([s_lat, s_ctx, sk], axis=-1), axis=-1)
    p_lat = p[..., :3 * WINDOW].astype(v.dtype)
    p_ctx = p[..., 3 * WINDOW:-1].astype(v.dtype)
    o = (jnp.einsum("bnkgql,bnlkd->bnqkgd", p_lat, v_band)
         + jnp.einsum("bnkgql,blkd->bnqkgd", p_ctx, v_ctx))
    return o.reshape(b, s, hq * hd)


def token_mixers(h, lp, ctx):
    b, s, _ = h.shape
    xa, ya, qb, kb, vb, qc, kc, vc, ga, gb, gc = split_in(h @ lp["w_in"])
    is_ctx = ctx is None
    xconv = dwconv(xa, lp["lru_conv_w"], lp["lru_conv_b"], LRU_CONV_PAD_L)
    h0 = jnp.zeros((b, 2, D_LRU), jnp.float32) if is_ctx else ctx["state"]
    h_f, last_f = rglru(xconv, lp["lru_wa"][0], lp["lru_ba"][0], lp["lru_wx"][0],
                        lp["lru_bx"][0], lp["lru_lam"][0], h0[:, 0], reverse=False)
    h_b, last_b = rglru(xconv, lp["lru_wa"][1], lp["lru_ba"][1], lp["lru_wx"][1],
                        lp["lru_bx"][1], lp["lru_lam"][1], h0[:, 1], reverse=True)
    o_a = (h_f + h_b) * jax.nn.gelu(ya)
    qb = rmsnorm(qb.reshape(b, s, HEADS_B, HEAD_DIM), lp["qnorm_g"])
    kb = rmsnorm(kb.reshape(b, s, KV_HEADS_B, HEAD_DIM), lp["knorm_g"])
    vb = vb.reshape(b, s, KV_HEADS_B, HEAD_DIM)
    qc = qc.reshape(b, s, HEADS_C, HEAD_DIM)
    kc = kc.reshape(b, s, KV_HEADS_C, HEAD_DIM)
    vc = vc.reshape(b, s, KV_HEADS_C, HEAD_DIM)
    if is_ctx:
        o_b = gqa_blocked(qb, kb, vb)
        o_c = gqa_blocked(qc, kc, vc, lp["sink_c"])
    else:
        o_b = gqa_blocked(axial_rope(qb),
                          jnp.concatenate([ctx["kb"], axial_rope(kb)], axis=1),
                          jnp.concatenate([ctx["vb"], vb], axis=1))
        o_c = window_attn_latent(axial_rope(qc), axial_rope(kc), vc,
                                 ctx["kc"], ctx["vc"], lp["sink_c"])
    merged = (jax.nn.sigmoid(ga) * (o_a @ lp["w_oa"])
              + jax.nn.sigmoid(gb) * (o_b @ lp["w_ob"])
              + jax.nn.sigmoid(gc) * (o_c @ lp["w_oc"]))
    out = merged @ lp["w_out"]
    if is_ctx:
        return out, (kb, vb, kc, vc, jnp.stack([last_f, last_b], axis=1))
    return out, None


def conv_ffn(h, lp):
    gate, val = jnp.split(h @ lp["w_up"], 2, axis=-1)
    gate = dwconv(gate, lp["ffn_conv_w"], lp["ffn_conv_b"], FFN_CONV_PAD_L)
    return (jax.nn.gelu(gate) * val) @ lp["w_down"]


def layer(x, cond, lp, ctx):
    sh1, sc1, g1, sh2, sc2, g2 = adaln_params(cond, lp["w_mod"], lp["b_mod"])
    mix, ctx_out = token_mixers(modulate(x, lp["norm1_g"], sh1, sc1), lp, ctx)
    x = x + g1 * mix
    x = x + g2 * conv_ffn(modulate(x, lp["norm2_g"], sh2, sc2), lp)
    return x, ctx_out


def setup_inputs(seed: int = 0) -> dict:
    key = jax.random.key(seed)
    ks = jax.random.split(key, 40)

    def nrm(k, shape, scale):
        return jax.random.normal(k, shape, jnp.float32) * scale

    a0 = jax.random.uniform(ks[20], (DEPTH, 2, D_LRU), jnp.float32, 0.9, 0.999)
    return {
        "x_prompt": nrm(ks[0], (BATCH, SEQ, D_MODEL), 1.0),
        "x_sample": nrm(ks[1], (DEC_BATCH, DEC_SEQ, D_MODEL), 1.0),
        "c": nrm(ks[2], (DEC_BATCH, D_MODEL), 1.0),
        "cache_kb": nrm(ks[3], (DEC_BATCH, DEPTH, PAST_LEN, KV_HEADS_B, HEAD_DIM), 1.0),
        "cache_vb": nrm(ks[4], (DEC_BATCH, DEPTH, PAST_LEN, KV_HEADS_B, HEAD_DIM), 1.0),
        "cache_kc": nrm(ks[5], (DEC_BATCH, DEPTH, PAST_LEN, KV_HEADS_C, HEAD_DIM), 1.0),
        "cache_vc": nrm(ks[6], (DEC_BATCH, DEPTH, PAST_LEN, KV_HEADS_C, HEAD_DIM), 1.0),
        "state_lru": nrm(ks[7], (DEC_BATCH, DEPTH, 2, D_LRU), 0.5),
        "c_ctx": nrm(ks[8], (D_MODEL,), 1.0),
        "norm1_g": 1.0 + nrm(ks[9], (DEPTH, D_MODEL), 0.02),
        "norm2_g": 1.0 + nrm(ks[10], (DEPTH, D_MODEL), 0.02),
        "w_mod": nrm(ks[11], (DEPTH, D_MODEL, 6 * D_MODEL), 0.5 * D_MODEL ** -0.5),
        "b_mod": nrm(ks[12], (DEPTH, 6 * D_MODEL), 0.02),
        "w_in": nrm(ks[13], (DEPTH, D_MODEL, D_IN), D_MODEL ** -0.5),
        "lru_conv_w": nrm(ks[14], (DEPTH, LRU_CONV_W, D_LRU), LRU_CONV_W ** -0.5),
        "lru_conv_b": nrm(ks[15], (DEPTH, D_LRU), 0.02),
        "lru_wa": nrm(ks[16], (DEPTH, 2, LRU_BLOCKS, LRU_BLOCK, LRU_BLOCK), LRU_BLOCK ** -0.5),
        "lru_ba": nrm(ks[17], (DEPTH, 2, D_LRU), 0.02),
        "lru_wx": nrm(ks[18], (DEPTH, 2, LRU_BLOCKS, LRU_BLOCK, LRU_BLOCK), LRU_BLOCK ** -0.5),
        "lru_bx": nrm(ks[19], (DEPTH, 2, D_LRU), 0.02),
        "lru_lam": jnp.log(a0) - jnp.log1p(-a0),
        "qnorm_g": 1.0 + nrm(ks[21], (DEPTH, HEAD_DIM), 0.02),
        "knorm_g": 1.0 + nrm(ks[22], (DEPTH, HEAD_DIM), 0.02),
        "sink_c": nrm(ks[23], (DEPTH, HEADS_C), 0.5),
        "w_oa": nrm(ks[24], (DEPTH, D_LRU, D_MODEL), D_LRU ** -0.5),
        "w_ob": nrm(ks[25], (DEPTH, D_QB, D_MODEL), D_QB ** -0.5),
        "w_oc": nrm(ks[26], (DEPTH, D_QC, D_MODEL), D_QC ** -0.5),
        "w_out": nrm(ks[27], (DEPTH, D_MODEL, D_MODEL), D_MODEL ** -0.5),
        "w_up": nrm(ks[28], (DEPTH, D_MODEL, 2 * D_FF), D_MODEL ** -0.5),
        "ffn_conv_w": nrm(ks[29], (DEPTH, FFN_CONV_W, D_FF), FFN_CONV_W ** -0.5),
        "ffn_conv_b": nrm(ks[30], (DEPTH, D_FF), 0.02),
        "w_down": nrm(ks[31], (DEPTH, D_FF, D_MODEL), D_FF ** -0.5),
        "final_g": 1.0 + nrm(ks[32], (D_MODEL,), 0.02),
    }


def reference(x_prompt, x_sample, c, cache_kb, cache_vb, cache_kc, cache_vc, state_lru,
              c_ctx, norm1_g, norm2_g, w_mod, b_mod, w_in, lru_conv_w, lru_conv_b,
              lru_wa, lru_ba, lru_wx, lru_bx, lru_lam, qnorm_g, knorm_g, sink_c,
              w_oa, w_ob, w_oc, w_out, w_up, ffn_conv_w, ffn_conv_b, w_down, final_g):
    stacked = {
        "norm1_g": norm1_g, "norm2_g": norm2_g, "w_mod": w_mod, "b_mod": b_mod,
        "w_in": w_in, "lru_conv_w": lru_conv_w, "lru_conv_b": lru_conv_b,
        "lru_wa": lru_wa, "lru_ba": lru_ba, "lru_wx": lru_wx, "lru_bx": lru_bx,
        "lru_lam": lru_lam, "qnorm_g": qnorm_g, "knorm_g": knorm_g, "sink_c": sink_c,
        "w_oa": w_oa, "w_ob": w_ob, "w_oc": w_oc, "w_out": w_out, "w_up": w_up,
        "ffn_conv_w": ffn_conv_w, "ffn_conv_b": ffn_conv_b, "w_down": w_down,
    }
    cond_ctx = c_ctx[None, :]
    xp, xs = x_prompt, x_sample
    kbs, vbs, kcs, vcs, lrus = [], [], [], [], []
    for l in range(DEPTH):
        lp = {name: arr[l] for name, arr in stacked.items()}
        xp, (kb_l, vb_l, kc_l, vc_l, lru_l) = layer(xp, cond_ctx, lp, None)
        kbs.append(kb_l)
        vbs.append(vb_l)
        kcs.append(kc_l)
        vcs.append(vc_l)
        lrus.append(lru_l)
        cached = {"kb": cache_kb[:, l], "vb": cache_vb[:, l], "kc": cache_kc[:, l],
                  "vc": cache_vc[:, l], "state": state_lru[:, l]}
        xs, _ = layer(xs, c, lp, cached)
    y_prompt = rmsnorm(xp, final_g)
    y_sample = rmsnorm(xs, final_g)
    new_kb = jnp.stack(kbs, axis=1)
    new_vb = jnp.stack(vbs, axis=1)
    new_kc = jnp.stack(kcs, axis=1)
    new_vc = jnp.stack(vcs, axis=1)
    new_lru = jnp.stack(lrus, axis=1)
    return (y_prompt, y_sample, new_kb, new_vb, new_kc, new_vc, new_lru)
```

```python
import functools

import jax
import jax.numpy as jnp
import numpy as np
from jax import lax
from jax.experimental import pallas as pl
from jax.experimental.pallas import tpu as pltpu

D_MODEL = 1024
DEPTH = 2
GRID_W = 64
D_LRU = 1024
LRU_BLOCKS = 8
LRU_BLOCK = D_LRU // LRU_BLOCKS
LRU_C = 8.0
HEAD_DIM = 64
N_HEADS = 8
N_KV = 2
D_Q = N_HEADS * HEAD_DIM
D_KV = N_KV * HEAD_DIM
WINDOW = 128
D_FF = 2816
ROPE_THETA = 10000.0
NORM_EPS = 1e-6
NEG_INF = -1e30
ATTN_SCALE = HEAD_DIM ** -0.5
IN_SIZES = (D_LRU, D_LRU, D_Q, D_KV, D_KV, D_Q, D_KV, D_KV, D_MODEL, D_MODEL, D_MODEL)
IN_OFF = tuple(int(v) for v in np.cumsum((0,) + IN_SIZES))
D_IN = IN_OFF[-1]

V7X_LANES = 128
V7X_SUBLANES_F32 = 8
V7X_SUBLANES_BF16 = 16
V7X_VMEM_LIMIT_BYTES = 56 * 1024 * 1024

TOKEN_TILE = 512
FFN_CHUNK = 256
ATTN_Q_TILE = 256
LRU_CHUNK = 512
MOD_COL_TILE = 1536

F32 = jnp.float32
BF16 = jnp.bfloat16


def _params(*sem):
    return pltpu.CompilerParams(dimension_semantics=sem, vmem_limit_bytes=V7X_VMEM_LIMIT_BYTES)


def _resident(shape):
    nd = len(shape)
    return pl.BlockSpec(shape, lambda *_: (0,) * nd, pipeline_mode=pl.Buffered(1))


def _sigmoid(x):
    return 1.0 / (1.0 + jnp.exp(-x))


def _gelu_tanh(x):
    return 0.5 * x * (1.0 + jnp.tanh(0.7978845608028654 * (x + 0.044715 * (x * x * x))))


def _rms_rows(x):
    return x * lax.rsqrt(jnp.mean(x * x, axis=-1, keepdims=True) + NORM_EPS)


def _dot(a, b):
    return jnp.dot(a, b, preferred_element_type=F32)


def _mod_kernel(c_ref, w_ref, b_ref, o_ref):
    c = c_ref[...]
    s = (c * _sigmoid(c)).astype(BF16)
    o_ref[0] = _dot(s, w_ref[0].astype(BF16)) + b_ref[0]


def _mod_call(cond8, w_mod, b_mod):
    depth, d, n = w_mod.shape
    tn = MOD_COL_TILE
    return pl.pallas_call(
        _mod_kernel,
        grid=(depth, n // tn),
        in_specs=[
            pl.BlockSpec((8, d), lambda l, j: (0, 0)),
            pl.BlockSpec((1, d, tn), lambda l, j: (l, 0, j)),
            pl.BlockSpec((1, 1, tn), lambda l, j: (l, 0, j)),
        ],
        out_specs=pl.BlockSpec((1, 8, tn), lambda l, j: (l, 0, j)),
        out_shape=jax.ShapeDtypeStruct((depth, 8, n), F32),
        compiler_params=_params("parallel", "parallel"),
        name="mod",
    )(cond8, w_mod, b_mod.reshape(depth, 1, n))


def _head_rms(x, seg_ref, g):
    sq = x * x
    hi = sq.astype(BF16)
    lo = (sq - hi.astype(F32)).astype(BF16)
    ms = _dot(hi, seg_ref[...]) + _dot(lo, seg_ref[...])
    return x * lax.rsqrt(ms + NORM_EPS) * g


def _rope(x, cos, sin):
    w = x.shape[1]
    lane = lax.broadcasted_iota(jnp.int32, x.shape, 1)
    fwd = pltpu.roll(x, w - 16, 1)
    bwd = pltpu.roll(x, 16, 1)
    partner = jnp.where((lane % 32) < 16, fwd, bwd)
    return x * cos + partner * sin


def _dup_heads(x):
    lane = lax.broadcasted_iota(jnp.int32, x.shape, 1)
    swapped = pltpu.roll(x, HEAD_DIM, 1)
    low = lane < HEAD_DIM
    return jnp.concatenate([jnp.where(low, x, swapped), jnp.where(low, swapped, x)], axis=1)


def _in_proj_kernel(*refs, rope, emit_kv):
    it = iter(refs)
    x_ref, mod_ref, g_ref, w_ref, qg_ref, kg_ref, seg_q_ref, seg_k_ref = (next(it) for _ in range(8))
    cos_ref = sin_ref = None
    if rope:
        cos_ref, sin_ref = next(it), next(it)
    (xa_ref, gya_ref, qb_ref, kbd_ref, vbd_ref, qc_ref, kcd_ref, vcd_ref,
     sga_ref, sgb_ref, sgc_ref) = (next(it) for _ in range(11))
    if emit_kv:
        kb_ref, vb_ref, kc_ref, vc_ref = (next(it) for _ in range(4))

    mod = mod_ref[0]
    shift, scale = mod[0:1], mod[1:2]
    h = (_rms_rows(x_ref[...]) * (g_ref[...] * (1.0 + scale)) + shift).astype(BF16)

    def seg(lo, hi):
        return _dot(h, w_ref[:, IN_OFF[lo]:IN_OFF[hi]])

    xa_ref[...] = seg(0, 1).astype(BF16)
    gya_ref[...] = _gelu_tanh(seg(1, 2)).astype(BF16)
    sga_ref[...] = _sigmoid(seg(8, 9)).astype(BF16)
    sgb_ref[...] = _sigmoid(seg(9, 10)).astype(BF16)
    sgc_ref[...] = _sigmoid(seg(10, 11)).astype(BF16)

    qb = _head_rms(seg(2, 3), seg_q_ref, qg_ref[...])
    kv_b = seg(3, 5)
    kb = _head_rms(kv_b[:, :D_KV], seg_k_ref, kg_ref[...])
    vb = kv_b[:, D_KV:]
    qkv_c = seg(5, 8)
    qc = qkv_c[:, :D_Q]
    kc = qkv_c[:, D_Q:D_Q + D_KV]
    vc = qkv_c[:, D_Q + D_KV:]
    if emit_kv:
        kb_ref[...] = kb
        vb_ref[...] = vb
        kc_ref[...] = kc
        vc_ref[...] = vc
    if rope:
        cos, sin = cos_ref[...], sin_ref[...]
        cos_q = jnp.concatenate([cos] * (D_Q // V7X_LANES), axis=1)
        sin_q = jnp.concatenate([sin] * (D_Q // V7X_LANES), axis=1)
        qb = _rope(qb, cos_q, sin_q)
        qc = _rope(qc, cos_q, sin_q)
        kb = _rope(kb, cos, sin)
        kc = _rope(kc, cos, sin)
    qb_ref[...] = (qb * ATTN_SCALE).astype(BF16)
    qc_ref[...] = (qc * ATTN_SCALE).astype(BF16)
    kbd_ref[...] = _dup_heads(kb).astype(BF16)
    vbd_ref[...] = _dup_heads(vb).astype(BF16)
    kcd_ref[...] = _dup_heads(kc).astype(BF16)
    vcd_ref[...] = _dup_heads(vc).astype(BF16)


def _in_proj_call(x, mod, g, w_in, qg, kg, seg_q, seg_k, rope_tabs, seq_len, emit_kv):
    ntok = x.shape[0]
    tm = TOKEN_TILE
    tiles_per_seq = max(seq_len // tm, 1)
    ncond = mod.shape[0]
    cond_of = (lambda i: i // tiles_per_seq) if ncond > 1 else (lambda i: 0)
    row = lambda w: pl.BlockSpec((tm, w), lambda i: (i, 0))
    in_specs = [
        row(D_MODEL),
        pl.BlockSpec((1, 6, D_MODEL), lambda i: (cond_of(i), 0, 0)),
        _resident((1, D_MODEL)),
        _resident((D_MODEL, D_IN)),
        _resident((1, D_Q)),
        _resident((1, D_KV)),
        _resident((D_Q, D_Q)),
        _resident((D_KV, D_KV)),
    ]
    args = [x, mod, g, w_in, qg, kg, seg_q, seg_k]
    rope = rope_tabs is not None
    if rope:
        in_specs += [pl.BlockSpec((tm, V7X_LANES), lambda i: (i % tiles_per_seq, 0))] * 2
        args += list(rope_tabs)
    widths = [D_LRU, D_LRU, D_Q, 2 * D_KV, 2 * D_KV, D_Q, 2 * D_KV, 2 * D_KV, D_MODEL, D_MODEL, D_MODEL]
    out_specs = [row(w) for w in widths]
    out_shape = [jax.ShapeDtypeStruct((ntok, w), BF16) for w in widths]
    if emit_kv:
        out_specs += [row(D_KV)] * 4
        out_shape += [jax.ShapeDtypeStruct((ntok, D_KV), F32)] * 4
    return pl.pallas_call(
        functools.partial(_in_proj_kernel, rope=rope, emit_kv=emit_kv),
        grid=(ntok // tm,),
        in_specs=in_specs,
        out_specs=out_specs,
        out_shape=out_shape,
        compiler_params=_params("parallel"),
        name="in_proj",
    )(*args)


def _lru_kernel(*refs, chunk, n_chunks, reverse):
    it = iter(refs)
    (cur_ref, prev_ref, next_ref, cw_ref, cb_ref, wg_ref, ba_ref, bx_ref, lam_ref,
     h0_ref) = (next(it) for _ in range(10))
    if reverse:
        hf_ref, gya_ref = next(it), next(it)
    out_ref, last_ref = next(it), next(it)
    ext_ref, a_ref, u_ref, hs_ref, state_ref = (next(it) for _ in range(5))

    j = pl.program_id(1)
    jj = (n_chunks - 1 - j) if reverse else j
    halo = V7X_SUBLANES_F32
    prev = prev_ref[...].astype(F32)[V7X_SUBLANES_BF16 - halo:]
    nxt = next_ref[...].astype(F32)[:halo]
    ext_ref[0:halo, :] = jnp.where(jj > 0, prev, 0.0)
    ext_ref[halo:halo + chunk, :] = cur_ref[...].astype(F32)
    ext_ref[halo + chunk:, :] = jnp.where(jj < n_chunks - 1, nxt, 0.0)

    cw = cw_ref[...]
    xc = cb_ref[...] + cw[0:1] * ext_ref[halo - 2:halo - 2 + chunk, :]
    for k in range(1, 4):
        xc = xc + cw[k:k + 1] * ext_ref[halo - 2 + k:halo - 2 + k + chunk, :]
    xc_b = xc.astype(BF16)

    lam = lam_ref[...]
    softplus_neg_lam = jnp.maximum(-lam, 0.0) + jnp.log1p(jnp.exp(-jnp.abs(lam)))
    decay = -LRU_C * softplus_neg_lam
    for n in range(LRU_BLOCKS):
        cols = slice(n * LRU_BLOCK, (n + 1) * LRU_BLOCK)
        z = _dot(xc_b[:, cols], wg_ref[n])
        r = _sigmoid(z[:, :LRU_BLOCK] + ba_ref[:, cols])
        i = _sigmoid(z[:, LRU_BLOCK:] + bx_ref[:, cols])
        log_a = decay[:, cols] * r
        a = jnp.exp(log_a)
        a_ref[:, cols] = a
        one_minus_a2 = -jnp.tanh(log_a) * (a * a + 1.0)
        u_ref[:, cols] = jnp.sqrt(one_minus_a2) * (i * xc[:, cols])

    @pl.when(j == 0)
    def _():
        state_ref[...] = h0_ref[0]

    def step(t, h):
        idx = (chunk - 1 - t) if reverse else t
        h = a_ref[pl.ds(idx, 1), :] * h + u_ref[pl.ds(idx, 1), :]
        hs_ref[pl.ds(idx, 1), :] = h
        return h

    h = lax.fori_loop(0, chunk, step, state_ref[...], unroll=8)
    state_ref[...] = h

    if reverse:
        out_ref[...] = ((hf_ref[...] + hs_ref[...]) * gya_ref[...].astype(F32)).astype(out_ref.dtype)
    else:
        out_ref[...] = hs_ref[...]

    @pl.when(j == n_chunks - 1)
    def _():
        last_ref[0] = h


def _lru_call(xa, conv_w, conv_b, w_gate, ba, bx, lam, h0, seq_len, reverse, hf=None, gya=None):
    ntok = xa.shape[0]
    nb = ntok // seq_len
    chunk = min(LRU_CHUNK, seq_len)
    nt = seq_len // chunk
    hb = V7X_SUBLANES_BF16
    n_halo_blocks = ntok // hb
    pos = (lambda j: nt - 1 - j) if reverse else (lambda j: j)
    cur = lambda b, j: (b * nt + pos(j), 0)
    prev = lambda b, j: (jnp.maximum((b * nt + pos(j)) * (chunk // hb) - 1, 0), 0)
    nxt = lambda b, j: (jnp.minimum((b * nt + pos(j) + 1) * (chunk // hb), n_halo_blocks - 1), 0)
    in_specs = [
        pl.BlockSpec((chunk, D_LRU), cur),
        pl.BlockSpec((hb, D_LRU), prev),
        pl.BlockSpec((hb, D_LRU), nxt),
        _resident((4, D_LRU)),
        _resident((1, D_LRU)),
        _resident((LRU_BLOCKS, LRU_BLOCK, 2 * LRU_BLOCK)),
        _resident((1, D_LRU)),
        _resident((1, D_LRU)),
        _resident((1, D_LRU)),
        pl.BlockSpec((1, 1, D_LRU), lambda b, j: (b, 0, 0)),
    ]
    args = [xa, xa, xa, conv_w, conv_b, w_gate, ba, bx, lam, h0]
    if reverse:
        in_specs += [pl.BlockSpec((chunk, D_LRU), cur)] * 2
        args += [hf, gya]
    return pl.pallas_call(
        functools.partial(_lru_kernel, chunk=chunk, n_chunks=nt, reverse=reverse),
        grid=(nb, nt),
        in_specs=in_specs,
        out_specs=[pl.BlockSpec((chunk, D_LRU), cur), pl.BlockSpec((1, 1, D_LRU), lambda b, j: (b, 0, 0))],
        out_shape=[jax.ShapeDtypeStruct((ntok, D_LRU), BF16 if reverse else F32),
                   jax.ShapeDtypeStruct((nb, 1, D_LRU), F32)],
        scratch_shapes=[
            pltpu.VMEM((chunk + 2 * V7X_SUBLANES_F32, D_LRU), F32),
            pltpu.VMEM((chunk, D_LRU), F32),
            pltpu.VMEM((chunk, D_LRU), F32),
            pltpu.VMEM((chunk, D_LRU), F32),
            pltpu.VMEM((1, D_LRU), F32),
        ],
        compiler_params=_params("arbitrary", "arbitrary"),
        name="lru_bwd" if reverse else "lru_fwd",
    )(*args)


def _attn_kernel(*refs, tq, seq_len, mode, has_ctx, has_sink):
    it = iter(refs)
    q_ref, k_ref, v_ref = next(it), next(it), next(it)
    ck_ref = cv_ref = sink_ref = None
    if has_ctx:
        ck_ref, cv_ref = next(it), next(it)
    if has_sink:
        sink_ref = next(it)
    o_ref = next(it)
    m_ref, l_ref, acc_ref = next(it), next(it), next(it)

    qi = pl.program_id(1)
    group = N_HEADS // N_KV
    rows = group * tq
    lane = lax.broadcasted_iota(jnp.int32, (tq, V7X_LANES), 1)
    low = lane < HEAD_DIM
    key_tile = tq

    def attend(qs, k, v, mask):
        s = lax.dot_general(qs, k, (((1,), (1,)), ((), ())), preferred_element_type=F32)
        if mask is not None:
            s = jnp.where(mask, s, NEG_INF)
        m_old = m_ref[...]
        m_new = jnp.maximum(m_old, jnp.max(s, axis=1, keepdims=True))
        alpha = jnp.exp(m_old - m_new)
        p = jnp.exp(s - m_new)
        l_ref[...] = alpha * l_ref[...] + jnp.sum(p, axis=1, keepdims=True)
        acc_ref[...] = alpha * acc_ref[...] + _dot(p.astype(BF16), v)
        m_ref[...] = m_new

    for g in range(N_KV):
        kv_cols = slice(g * V7X_LANES, (g + 1) * V7X_LANES)
        parts = []
        for p in range(group // 2):
            qp = q_ref[:, (g * (group // 2) + p) * V7X_LANES:(g * (group // 2) + p + 1) * V7X_LANES]
            zero = jnp.zeros_like(qp)
            parts += [jnp.where(low, qp, zero), jnp.where(low, zero, qp)]
        qs = jnp.concatenate(parts, axis=0)

        if has_sink:
            m_ref[...] = jnp.concatenate(
                [jnp.full((tq, 1), sink_ref[g * group + h], F32) for h in range(group)], axis=0)
            l_ref[...] = jnp.ones((rows, 1), F32)
        else:
            m_ref[...] = jnp.full((rows, 1), NEG_INF, F32)
            l_ref[...] = jnp.zeros((rows, 1), F32)
        acc_ref[...] = jnp.zeros((rows, V7X_LANES), F32)

        if has_ctx:
            attend(qs, ck_ref[:, kv_cols], cv_ref[:, kv_cols], None)

        if mode == "full":
            n_tiles = seq_len // key_tile
            if n_tiles == 1:
                attend(qs, k_ref[:, kv_cols], v_ref[:, kv_cols], None)
            else:
                def body(t, carry):
                    start = pl.multiple_of(t * key_tile, key_tile)
                    attend(qs, k_ref[pl.ds(start, key_tile), kv_cols], v_ref[pl.ds(start, key_tile), kv_cols], None)
                    return carry
                lax.fori_loop(0, n_tiles, body, 0)
        else:
            span = tq + 2 * WINDOW
            q0 = qi * tq
            start = pl.multiple_of(jnp.clip(q0 - WINDOW, 0, seq_len - span), WINDOW)
            q_abs = q0 + lax.broadcasted_iota(jnp.int32, (tq, span), 0)
            k_abs = start + lax.broadcasted_iota(jnp.int32, (tq, span), 1)
            near = jnp.abs(q_abs - k_abs) <= WINDOW
            mask = jnp.concatenate([near] * group, axis=0)
            attend(qs, k_ref[pl.ds(start, span), kv_cols], v_ref[pl.ds(start, span), kv_cols], mask)

        o = acc_ref[...] / l_ref[...]
        for p in range(group // 2):
            even = o[(2 * p) * tq:(2 * p + 1) * tq]
            odd = o[(2 * p + 1) * tq:(2 * p + 2) * tq]
            col = (g * (group // 2) + p) * V7X_LANES
            o_ref[:, col:col + V7X_LANES] = jnp.where(low, even, odd).astype(o_ref.dtype)


def _attn_call(q, kd, vd, seq_len, mode, ctx=None, sink=None):
    ntok = q.shape[0]
    nb = ntok // seq_len
    tq = min(ATTN_Q_TILE, seq_len)
    nq = seq_len // tq
    group = N_HEADS // N_KV
    in_specs = [
        pl.BlockSpec((tq, D_Q), lambda b, i: (b * nq + i, 0)),
        pl.BlockSpec((seq_len, 2 * D_KV), lambda b, i: (b, 0)),
        pl.BlockSpec((seq_len, 2 * D_KV), lambda b, i: (b, 0)),
    ]
    args = [q, kd, vd]
    if ctx is not None:
        past = ctx[0].shape[1]
        in_specs += [pl.BlockSpec((None, past, 2 * D_KV), lambda b, i: (b, 0, 0))] * 2
        args += list(ctx)
    if sink is not None:
        in_specs.append(pl.BlockSpec(memory_space=pltpu.SMEM))
        args.append(sink)
    return pl.pallas_call(
        functools.partial(_attn_kernel, tq=tq, seq_len=seq_len, mode=mode,
                          has_ctx=ctx is not None, has_sink=sink is not None),
        grid=(nb, nq),
        in_specs=in_specs,
        out_specs=pl.BlockSpec((tq, D_Q), lambda b, i: (b * nq + i, 0)),
        out_shape=jax.ShapeDtypeStruct((ntok, D_Q), BF16),
        scratch_shapes=[
            pltpu.VMEM((group * tq, 1), F32),
            pltpu.VMEM((group * tq, 1), F32),
            pltpu.VMEM((group * tq, V7X_LANES), F32),
        ],
        compiler_params=_params("parallel", "parallel"),
        name="attn_" + mode + ("_sink" if sink is not None else ""),
    )(*args)


def _mix_out_kernel(oa_ref, ob_ref, oc_ref, sga_ref, sgb_ref, sgc_ref, x_ref, mod_ref,
                    woa_ref, wob_ref, woc_ref, wout_ref, o_ref):
    merged = (sga_ref[...].astype(F32) * _dot(oa_ref[...], woa_ref[...])
              + sgb_ref[...].astype(F32) * _dot(ob_ref[...], wob_ref[...])
              + sgc_ref[...].astype(F32) * _dot(oc_ref[...], woc_ref[...]))
    out = _dot(merged.astype(BF16), wout_ref[...])
    o_ref[...] = x_ref[...] + mod_ref[0][2:3] * out


def _mix_out_call(oa, ob, oc, sga, sgb, sgc, x, mod, w_oa, w_ob, w_oc, w_out, seq_len):
    ntok = x.shape[0]
    tm = TOKEN_TILE
    tiles_per_seq = max(seq_len // tm, 1)
    cond_of = (lambda i: i // tiles_per_seq) if mod.shape[0] > 1 else (lambda i: 0)
    row = lambda w: pl.BlockSpec((tm, w), lambda i: (i, 0))
    return pl.pallas_call(
        _mix_out_kernel,
        grid=(ntok // tm,),
        in_specs=[row(D_LRU), row(D_Q), row(D_Q), row(D_MODEL), row(D_MODEL), row(D_MODEL), row(D_MODEL),
                  pl.BlockSpec((1, 6, D_MODEL), lambda i: (cond_of(i), 0, 0)),
                  _resident((D_LRU, D_MODEL)), _resident((D_Q, D_MODEL)), _resident((D_Q, D_MODEL)),
                  _resident((D_MODEL, D_MODEL))],
        out_specs=row(D_MODEL),
        out_shape=jax.ShapeDtypeStruct((ntok, D_MODEL), F32),
        compiler_params=_params("parallel"),
        name="mix_out",
    )(oa, ob, oc, sga, sgb, sgc, x, mod, w_oa, w_ob, w_oc, w_out)


def _ffn_kernel(*refs, tm, seq_len, final):
    it = iter(refs)
    x_ref, prev_ref, next_ref, mod_ref, g_ref, wup_ref, cw_ref, cb_ref, wdn_ref = (next(it) for _ in range(9))
    fg_ref = next(it) if final else None
    o_ref = next(it)
    hext_ref, gate_ref, act_ref = next(it), next(it), next(it)

    halo = V7X_SUBLANES_F32
    mod = mod_ref[0]
    shift, scale, gate2 = mod[3:4], mod[4:5], mod[5:6]
    gain = g_ref[...] * (1.0 + scale)
    x = x_ref[...]

    def modulated(v):
        return (_rms_rows(v) * gain + shift).astype(BF16)

    hext_ref[0:halo, :] = modulated(prev_ref[...])
    hext_ref[halo:halo + tm, :] = modulated(x)
    hext_ref[halo + tm:, :] = modulated(next_ref[...])

    pos = (pl.program_id(0) * tm + lax.broadcasted_iota(jnp.int32, (tm, 1), 0)) % seq_len
    has_prev = (pos != 0).astype(F32)
    has_next = (pos != seq_len - 1).astype(F32)

    cw = cw_ref[...]
    for c in range(D_FF // FFN_CHUNK):
        cols = slice(c * FFN_CHUNK, (c + 1) * FFN_CHUNK)
        vcols = slice(D_FF + c * FFN_CHUNK, D_FF + (c + 1) * FFN_CHUNK)
        gate_ref[...] = _dot(hext_ref[...], wup_ref[:, cols])
        val = _dot(hext_ref[halo:halo + tm, :], wup_ref[:, vcols])
        gate = (cb_ref[:, cols]
                + cw[0:1, cols] * (gate_ref[halo - 1:halo - 1 + tm, :] * has_prev)
                + cw[1:2, cols] * gate_ref[halo:halo + tm, :]
                + cw[2:3, cols] * (gate_ref[halo + 1:halo + 1 + tm, :] * has_next))
        act_ref[:, cols] = (_gelu_tanh(gate) * val).astype(BF16)

    y = x + gate2 * _dot(act_ref[...], wdn_ref[...])
    if final:
        y = _rms_rows(y) * fg_ref[...]
    o_ref[...] = y


def _ffn_call(x, mod, g, w_up, conv_w, conv_b, w_down, seq_len, final_g=None):
    ntok = x.shape[0]
    tm = TOKEN_TILE
    halo = V7X_SUBLANES_F32
    tiles_per_seq = max(seq_len // tm, 1)
    cond_of = (lambda i: i // tiles_per_seq) if mod.shape[0] > 1 else (lambda i: 0)
    n_halo_blocks = ntok // halo
    in_specs = [
        pl.BlockSpec((tm, D_MODEL), lambda i: (i, 0)),
        pl.BlockSpec((halo, D_MODEL), lambda i: (jnp.maximum(i * (tm // halo) - 1, 0), 0)),
        pl.BlockSpec((halo, D_MODEL), lambda i: (jnp.minimum((i + 1) * (tm // halo), n_halo_blocks - 1), 0)),
        pl.BlockSpec((1, 6, D_MODEL), lambda i: (cond_of(i), 0, 0)),
        _resident((1, D_MODEL)),
        _resident((D_MODEL, 2 * D_FF)),
        _resident((3, D_FF)),
        _resident((1, D_FF)),
        _resident((D_FF, D_MODEL)),
    ]
    args = [x, x, x, mod, g, w_up, conv_w, conv_b, w_down]
    final = final_g is not None
    if final:
        in_specs.append(_resident((1, D_MODEL)))
        args.append(final_g)
    return pl.pallas_call(
        functools.partial(_ffn_kernel, tm=tm, seq_len=seq_len, final=final),
        grid=(ntok // tm,),
        in_specs=in_specs,
        out_specs=pl.BlockSpec((tm, D_MODEL), lambda i: (i, 0)),
        out_shape=jax.ShapeDtypeStruct((ntok, D_MODEL), F32),
        scratch_shapes=[
            pltpu.VMEM((tm + 2 * halo, D_MODEL), BF16),
            pltpu.VMEM((tm + 2 * halo, FFN_CHUNK), F32),
            pltpu.VMEM((tm, D_FF), BF16),
        ],
        compiler_params=_params("parallel"),
        name="ffn",
    )(*args)


def _rope_tables(seq_len):
    nf = HEAD_DIM // 4
    inv = ROPE_THETA ** (-jnp.arange(nf, dtype=F32) / nf)
    t = jnp.arange(seq_len)
    ang_row = (t // GRID_W).astype(F32)[:, None] * inv[None, :]
    ang_col = (t % GRID_W).astype(F32)[:, None] * inv[None, :]
    cos = jnp.concatenate([jnp.cos(ang_row)] * 2 + [jnp.cos(ang_col)] * 2, axis=1)
    sin = jnp.concatenate([-jnp.sin(ang_row), jnp.sin(ang_row), -jnp.sin(ang_col), jnp.sin(ang_col)], axis=1)
    reps = V7X_LANES // HEAD_DIM
    return jnp.concatenate([cos] * reps, axis=1), jnp.concatenate([sin] * reps, axis=1)


def _segment_mean_matrix(width):
    seg = np.arange(width) // HEAD_DIM
    return jnp.asarray((seg[:, None] == seg[None, :]).astype(np.float32) / HEAD_DIM, dtype=BF16)


def _dup_cache(c):
    return jnp.concatenate([c[:, :, 0], c[:, :, 0], c[:, :, 1], c[:, :, 1]], axis=-1).astype(BF16)


def kernel(x_prompt, x_sample, c, cache_kb, cache_vb, cache_kc, cache_vc, state_lru, c_ctx, norm1_g, norm2_g, w_mod, b_mod, w_in, lru_conv_w, lru_conv_b, lru_wa, lru_ba, lru_wx, lru_bx, lru_lam, qnorm_g, knorm_g, sink_c, w_oa, w_ob, w_oc, w_out, w_up, ffn_conv_w, ffn_conv_b, w_down, final_g):
    batch, seq, d = x_prompt.shape
    dec_batch, dec_seq, _ = x_sample.shape
    depth = w_in.shape[0]
    assert d == D_MODEL and depth == DEPTH and dec_batch + 1 <= 8

    cond8 = jnp.zeros((8, d), F32).at[0].set(c_ctx).at[1:1 + dec_batch].set(c)
    mod_all = _mod_call(cond8, w_mod, b_mod).reshape(depth, 8, 6, d)

    w_in_b, w_oa_b, w_ob_b, w_oc_b = (w.astype(BF16) for w in (w_in, w_oa, w_ob, w_oc))
    w_out_b, w_up_b, w_down_b = (w.astype(BF16) for w in (w_out, w_up, w_down))
    w_gate = jnp.concatenate([lru_wa, lru_wx], axis=-1).astype(BF16)
    seg_q, seg_k = _segment_mean_matrix(D_Q), _segment_mean_matrix(D_KV)
    rope_tabs = _rope_tables(dec_seq)
    qg = jnp.tile(qnorm_g, (1, N_HEADS))
    kg = jnp.tile(knorm_g, (1, N_KV))

    def layer(l, x, mod, seq_len, h0, ctx, is_last):
        is_ctx = ctx is None
        outs = _in_proj_call(x, mod, norm1_g[l][None], w_in_b[l], qg[l][None], kg[l][None], seg_q, seg_k,
                             None if is_ctx else rope_tabs, seq_len, emit_kv=is_ctx)
        xa, gya, qb, kbd, vbd, qc, kcd, vcd, sga, sgb, sgc = outs[:11]
        lru_args = lambda dr: (lru_conv_w[l], lru_conv_b[l][None], w_gate[l, dr], lru_ba[l, dr][None],
                               lru_bx[l, dr][None], lru_lam[l, dr][None], h0[:, dr][:, None, :], seq_len)
        hf, last_f = _lru_call(xa, *lru_args(0), reverse=False)
        oa, last_b = _lru_call(xa, *lru_args(1), reverse=True, hf=hf, gya=gya)
        if is_ctx:
            ob = _attn_call(qb, kbd, vbd, seq_len, "full")
            oc = _attn_call(qc, kcd, vcd, seq_len, "full", sink=sink_c[l])
        else:
            ob = _attn_call(qb, kbd, vbd, seq_len, "full", ctx=(ctx["kb"], ctx["vb"]))
            oc = _attn_call(qc, kcd, vcd, seq_len, "window", ctx=(ctx["kc"], ctx["vc"]), sink=sink_c[l])
        x = _mix_out_call(oa, ob, oc, sga, sgb, sgc, x, mod, w_oa_b[l], w_ob_b[l], w_oc_b[l], w_out_b[l], seq_len)
        x = _ffn_call(x, mod, norm2_g[l][None], w_up_b[l], ffn_conv_w[l], ffn_conv_b[l][None], w_down_b[l],
                      seq_len, final_g=final_g[None] if is_last else None)
        return x, outs[11:], (last_f, last_b)

    xp = x_prompt.reshape(batch * seq, d)
    xs = x_sample.reshape(dec_batch * dec_seq, d)
    zeros_h0 = jnp.zeros((batch, 2, D_LRU), F32)
    kbs, vbs, kcs, vcs, lrus = [], [], [], [], []
    for l in range(depth):
        is_last = l == depth - 1
        xp, (kb, vb, kc, vc), (last_f, last_b) = layer(l, xp, mod_all[l, 0:1], seq, zeros_h0, None, is_last)
        kbs.append(kb)
        vbs.append(vb)
        kcs.append(kc)
        vcs.append(vc)
        lrus.append(jnp.concatenate([last_f, last_b], axis=1))
        cached = {"kb": _dup_cache(cache_kb[:, l]), "vb": _dup_cache(cache_vb[:, l]),
                  "kc": _dup_cache(cache_kc[:, l]), "vc": _dup_cache(cache_vc[:, l])}
        xs, _, _ = layer(l, xs, mod_all[l, 1:1 + dec_batch], dec_seq, state_lru[:, l], cached, is_last)

    def stack_kv(parts):
        return jnp.stack([p.reshape(batch, seq, N_KV, HEAD_DIM) for p in parts], axis=1)

    y_prompt = xp.reshape(batch, seq, d)
    y_sample = xs.reshape(dec_batch, dec_seq, d)
    return (y_prompt, y_sample, stack_kv(kbs), stack_kv(vbs), stack_kv(kcs), stack_kv(vcs),
            jnp.stack(lrus, axis=1))
```

```python
import functools

import jax
import jax.numpy as jnp
import numpy as np
from jax import lax
from jax.experimental import pallas as pl
from jax.experimental.pallas import tpu as pltpu

D_MODEL = 1024
DEPTH = 2
GRID_W = 64
D_LRU = 1024
LRU_BLOCKS = 8
LRU_BLOCK = D_LRU // LRU_BLOCKS
LRU_C = 8.0
HEAD_DIM = 64
N_HEADS = 8
N_KV = 2
D_Q = N_HEADS * HEAD_DIM
D_KV = N_KV * HEAD_DIM
WINDOW = 128
D_FF = 2816
ROPE_THETA = 10000.0
NORM_EPS = 1e-6
NEG_INF = -1e30
ATTN_SCALE = HEAD_DIM ** -0.5
LOG2_E = 1.4426950408889634
IN_SIZES = (D_LRU, D_LRU, D_Q, D_KV, D_KV, D_Q, D_KV, D_KV, D_MODEL, D_MODEL, D_MODEL)
IN_OFF = tuple(int(v) for v in np.cumsum((0,) + IN_SIZES))
D_IN = IN_OFF[-1]

V7X_LANES = 128
V7X_SUBLANES_F32 = 8
V7X_SUBLANES_BF16 = 16
V7X_VMEM_LIMIT_BYTES = 56 * 1024 * 1024

TOKEN_TILE = 512
FFN_CHUNK = 256
ATTN_Q_TILE = 256
ATTN_KEY_TILE = 256
ATTN_SEQS_PER_STEP = 4
ATTN_TILES_PER_STEP = 2
ATTN_LOOKAHEAD = 3
ATTN_ONES = 16
LRU_CHUNK = 512
MOD_COL_TILE = 1536

F32 = jnp.float32
BF16 = jnp.bfloat16


def _params(*sem):
    return pltpu.CompilerParams(dimension_semantics=sem, vmem_limit_bytes=V7X_VMEM_LIMIT_BYTES)


def _resident(shape):
    nd = len(shape)
    return pl.BlockSpec(shape, lambda *_: (0,) * nd, pipeline_mode=pl.Buffered(1))


def _sigmoid(x):
    return 1.0 / (1.0 + jnp.exp(-x))


def _gelu_tanh(x):
    return 0.5 * x * (1.0 + jnp.tanh(0.7978845608028654 * (x + 0.044715 * (x * x * x))))


def _rms_rows(x):
    return x * lax.rsqrt(jnp.mean(x * x, axis=-1, keepdims=True) + NORM_EPS)


def _dot(a, b):
    return jnp.dot(a, b, preferred_element_type=F32)


def _mod_kernel(c_ref, w_ref, b_ref, o_ref):
    c = c_ref[...]
    s = (c * _sigmoid(c)).astype(BF16)
    o_ref[0] = _dot(s, w_ref[0].astype(BF16)) + b_ref[0]


def _mod_call(cond8, w_mod, b_mod):
    depth, d, n = w_mod.shape
    tn = MOD_COL_TILE
    return pl.pallas_call(
        _mod_kernel,
        grid=(depth, n // tn),
        in_specs=[
            pl.BlockSpec((8, d), lambda l, j: (0, 0)),
            pl.BlockSpec((1, d, tn), lambda l, j: (l, 0, j)),
            pl.BlockSpec((1, 1, tn), lambda l, j: (l, 0, j)),
        ],
        out_specs=pl.BlockSpec((1, 8, tn), lambda l, j: (l, 0, j)),
        out_shape=jax.ShapeDtypeStruct((depth, 8, n), F32),
        compiler_params=_params("parallel", "parallel"),
        name="mod",
    )(cond8, w_mod, b_mod.reshape(depth, 1, n))


def _head_rms(x, seg_ref, g):
    sq = x * x
    hi = sq.astype(BF16)
    lo = (sq - hi.astype(F32)).astype(BF16)
    ms = _dot(hi, seg_ref[...]) + _dot(lo, seg_ref[...])
    return x * lax.rsqrt(ms + NORM_EPS) * g


def _rope(x, cos, sin):
    w = x.shape[1]
    lane = lax.broadcasted_iota(jnp.int32, x.shape, 1)
    fwd = pltpu.roll(x, w - 16, 1)
    bwd = pltpu.roll(x, 16, 1)
    partner = jnp.where((lane % 32) < 16, fwd, bwd)
    return x * cos + partner * sin


def _dup_heads(x):
    lane = lax.broadcasted_iota(jnp.int32, x.shape, 1)
    swapped = pltpu.roll(x, HEAD_DIM, 1)
    low = lane < HEAD_DIM
    return jnp.concatenate([jnp.where(low, x, swapped), jnp.where(low, swapped, x)], axis=1)


def _ext_values(x):
    lane = lax.broadcasted_iota(jnp.int32, x.shape, 1)
    swapped = pltpu.roll(x, HEAD_DIM, 1)
    low = lane < HEAD_DIM
    tail = jnp.where(lane < HEAD_DIM + ATTN_ONES, 1.0, 0.0)
    return jnp.concatenate([jnp.where(low, x, tail), jnp.where(low, swapped, tail)], axis=1)


def _in_proj_kernel(*refs, rope, emit_kv):
    it = iter(refs)
    x_ref, mod_ref, g_ref, w_ref, qg_ref, kg_ref, seg_q_ref, seg_k_ref = (next(it) for _ in range(8))
    cos_ref = sin_ref = None
    if rope:
        cos_ref, sin_ref = next(it), next(it)
    (xa_ref, gya_ref, qb_ref, kbd_ref, vbd_ref, qc_ref, kcd_ref, vcd_ref,
     sga_ref, sgb_ref, sgc_ref) = (next(it) for _ in range(11))
    if emit_kv:
        kb_ref, vb_ref, kc_ref, vc_ref = (next(it) for _ in range(4))

    mod = mod_ref[0]
    shift, scale = mod[0:1], mod[1:2]
    h = (_rms_rows(x_ref[...]) * (g_ref[...] * (1.0 + scale)) + shift).astype(BF16)

    def seg(lo, hi):
        return _dot(h, w_ref[:, IN_OFF[lo]:IN_OFF[hi]])

    xa_ref[...] = seg(0, 1).astype(BF16)
    gya_ref[...] = _gelu_tanh(seg(1, 2)).astype(BF16)
    sga_ref[...] = _sigmoid(seg(8, 9)).astype(BF16)
    sgb_ref[...] = _sigmoid(seg(9, 10)).astype(BF16)
    sgc_ref[...] = _sigmoid(seg(10, 11)).astype(BF16)

    qb = _head_rms(seg(2, 3), seg_q_ref, qg_ref[...])
    kv_b = seg(3, 5)
    kb = _head_rms(kv_b[:, :D_KV], seg_k_ref, kg_ref[...])
    vb = kv_b[:, D_KV:]
    qkv_c = seg(5, 8)
    qc = qkv_c[:, :D_Q]
    kc = qkv_c[:, D_Q:D_Q + D_KV]
    vc = qkv_c[:, D_Q + D_KV:]
    if emit_kv:
        kb_ref[...] = kb
        vb_ref[...] = vb
        kc_ref[...] = kc
        vc_ref[...] = vc
    if rope:
        cos, sin = cos_ref[...], sin_ref[...]
        cos_q = jnp.concatenate([cos] * (D_Q // V7X_LANES), axis=1)
        sin_q = jnp.concatenate([sin] * (D_Q // V7X_LANES), axis=1)
        qb = _rope(qb, cos_q, sin_q)
        qc = _rope(qc, cos_q, sin_q)
        kb = _rope(kb, cos, sin)
        kc = _rope(kc, cos, sin)
    qb_ref[...] = (qb * (ATTN_SCALE * LOG2_E)).astype(BF16)
    qc_ref[...] = (qc * (ATTN_SCALE * LOG2_E)).astype(BF16)
    kbd_ref[...] = _dup_heads(kb).astype(BF16)
    vbd_ref[...] = _ext_values(vb).astype(BF16)
    kcd_ref[...] = _dup_heads(kc).astype(BF16)
    vcd_ref[...] = _ext_values(vc).astype(BF16)


def _in_proj_call(x, mod, g, w_in, qg, kg, seg_q, seg_k, rope_tabs, seq_len, emit_kv):
    ntok = x.shape[0]
    tm = TOKEN_TILE
    tiles_per_seq = max(seq_len // tm, 1)
    ncond = mod.shape[0]
    cond_of = (lambda i: i // tiles_per_seq) if ncond > 1 else (lambda i: 0)
    row = lambda w: pl.BlockSpec((tm, w), lambda i: (i, 0))
    in_specs = [
        row(D_MODEL),
        pl.BlockSpec((1, 6, D_MODEL), lambda i: (cond_of(i), 0, 0)),
        _resident((1, D_MODEL)),
        _resident((D_MODEL, D_IN)),
        _resident((1, D_Q)),
        _resident((1, D_KV)),
        _resident((D_Q, D_Q)),
        _resident((D_KV, D_KV)),
    ]
    args = [x, mod, g, w_in, qg, kg, seg_q, seg_k]
    rope = rope_tabs is not None
    if rope:
        in_specs += [pl.BlockSpec((tm, V7X_LANES), lambda i: (i % tiles_per_seq, 0))] * 2
        args += list(rope_tabs)
    widths = [D_LRU, D_LRU, D_Q, 2 * D_KV, 2 * D_KV, D_Q, 2 * D_KV, 2 * D_KV, D_MODEL, D_MODEL, D_MODEL]
    out_specs = [row(w) for w in widths]
    out_shape = [jax.ShapeDtypeStruct((ntok, w), BF16) for w in widths]
    if emit_kv:
        out_specs += [row(D_KV)] * 4
        out_shape += [jax.ShapeDtypeStruct((ntok, D_KV), F32)] * 4
    return pl.pallas_call(
        functools.partial(_in_proj_kernel, rope=rope, emit_kv=emit_kv),
        grid=(ntok // tm,),
        in_specs=in_specs,
        out_specs=out_specs,
        out_shape=out_shape,
        compiler_params=_params("parallel"),
        name="in_proj",
    )(*args)


def _lru_kernel(*refs, chunk, n_chunks, reverse):
    it = iter(refs)
    (cur_ref, prev_ref, next_ref, cw_ref, cb_ref, wg_ref, ba_ref, bx_ref, lam_ref,
     h0_ref) = (next(it) for _ in range(10))
    if reverse:
        hf_ref, gya_ref = next(it), next(it)
    out_ref, last_ref = next(it), next(it)
    ext_ref, a_ref, u_ref, hs_ref, state_ref = (next(it) for _ in range(5))

    j = pl.program_id(1)
    jj = (n_chunks - 1 - j) if reverse else j
    halo = V7X_SUBLANES_F32
    prev = prev_ref[...].astype(F32)[V7X_SUBLANES_BF16 - halo:]
    nxt = next_ref[...].astype(F32)[:halo]
    ext_ref[0:halo, :] = jnp.where(jj > 0, prev, 0.0)
    ext_ref[halo:halo + chunk, :] = cur_ref[...].astype(F32)
    ext_ref[halo + chunk:, :] = jnp.where(jj < n_chunks - 1, nxt, 0.0)

    cw = cw_ref[...]
    xc = cb_ref[...] + cw[0:1] * ext_ref[halo - 2:halo - 2 + chunk, :]
    for k in range(1, 4):
        xc = xc + cw[k:k + 1] * ext_ref[halo - 2 + k:halo - 2 + k + chunk, :]
    xc_b = xc.astype(BF16)

    lam = lam_ref[...]
    softplus_neg_lam = jnp.maximum(-lam, 0.0) + jnp.log1p(jnp.exp(-jnp.abs(lam)))
    decay = -LRU_C * softplus_neg_lam
    for n in range(LRU_BLOCKS):
        cols = slice(n * LRU_BLOCK, (n + 1) * LRU_BLOCK)
        z = _dot(xc_b[:, cols], wg_ref[n])
        r = _sigmoid(z[:, :LRU_BLOCK] + ba_ref[:, cols])
        i = _sigmoid(z[:, LRU_BLOCK:] + bx_ref[:, cols])
        log_a = decay[:, cols] * r
        a = jnp.exp(log_a)
        a_ref[:, cols] = a
        one_minus_a2 = -jnp.tanh(log_a) * (a * a + 1.0)
        u_ref[:, cols] = jnp.sqrt(one_minus_a2) * (i * xc[:, cols])

    @pl.when(j == 0)
    def _():
        state_ref[...] = h0_ref[0]

    def step(t, h):
        idx = (chunk - 1 - t) if reverse else t
        h = a_ref[pl.ds(idx, 1), :] * h + u_ref[pl.ds(idx, 1), :]
        hs_ref[pl.ds(idx, 1), :] = h
        return h

    h = lax.fori_loop(0, chunk, step, state_ref[...], unroll=8)
    state_ref[...] = h

    if reverse:
        out_ref[...] = ((hf_ref[...] + hs_ref[...]) * gya_ref[...].astype(F32)).astype(out_ref.dtype)
    else:
        out_ref[...] = hs_ref[...]

    @pl.when(j == n_chunks - 1)
    def _():
        last_ref[0] = h


def _lru_call(xa, conv_w, conv_b, w_gate, ba, bx, lam, h0, seq_len, reverse, hf=None, gya=None):
    ntok = xa.shape[0]
    nb = ntok // seq_len
    chunk = min(LRU_CHUNK, seq_len)
    nt = seq_len // chunk
    hb = V7X_SUBLANES_BF16
    n_halo_blocks = ntok // hb
    pos = (lambda j: nt - 1 - j) if reverse else (lambda j: j)
    cur = lambda b, j: (b * nt + pos(j), 0)
    prev = lambda b, j: (jnp.maximum((b * nt + pos(j)) * (chunk // hb) - 1, 0), 0)
    nxt = lambda b, j: (jnp.minimum((b * nt + pos(j) + 1) * (chunk // hb), n_halo_blocks - 1), 0)
    in_specs = [
        pl.BlockSpec((chunk, D_LRU), cur),
        pl.BlockSpec((hb, D_LRU), prev),
        pl.BlockSpec((hb, D_LRU), nxt),
        _resident((4, D_LRU)),
        _resident((1, D_LRU)),
        _resident((LRU_BLOCKS, LRU_BLOCK, 2 * LRU_BLOCK)),
        _resident((1, D_LRU)),
        _resident((1, D_LRU)),
        _resident((1, D_LRU)),
        pl.BlockSpec((1, 1, D_LRU), lambda b, j: (b, 0, 0)),
    ]
    args = [xa, xa, xa, conv_w, conv_b, w_gate, ba, bx, lam, h0]
    if reverse:
        in_specs += [pl.BlockSpec((chunk, D_LRU), cur)] * 2
        args += [hf, gya]
    return pl.pallas_call(
        functools.partial(_lru_kernel, chunk=chunk, n_chunks=nt, reverse=reverse),
        grid=(nb, nt),
        in_specs=in_specs,
        out_specs=[pl.BlockSpec((chunk, D_LRU), cur), pl.BlockSpec((1, 1, D_LRU), lambda b, j: (b, 0, 0))],
        out_shape=[jax.ShapeDtypeStruct((ntok, D_LRU), BF16 if reverse else F32),
                   jax.ShapeDtypeStruct((nb, 1, D_LRU), F32)],
        scratch_shapes=[
            pltpu.VMEM((chunk + 2 * V7X_SUBLANES_F32, D_LRU), F32),
            pltpu.VMEM((chunk, D_LRU), F32),
            pltpu.VMEM((chunk, D_LRU), F32),
            pltpu.VMEM((chunk, D_LRU), F32),
            pltpu.VMEM((1, D_LRU), F32),
        ],
        compiler_params=_params("arbitrary", "arbitrary"),
        name="lru_bwd" if reverse else "lru_fwd",
    )(*args)


def _attn_kernel(*refs, tq, seq_len, n_seq, mode, has_ctx, has_sink):
    it = iter(refs)
    q_ref, k_ref, v_ref = next(it), next(it), next(it)
    ck_ref = cv_ref = sink_ref = None
    if has_ctx:
        ck_ref, cv_ref = next(it), next(it)
    if has_sink:
        sink_ref = next(it)
    o_ref = next(it)
    m_ref, acc_ref = next(it), next(it)

    qi = pl.program_id(1)
    group = N_HEADS // N_KV
    acc_rows = HEAD_DIM + ATTN_ONES
    acc_row = lax.broadcasted_iota(jnp.int32, (acc_rows, tq), 0)
    lane = lax.broadcasted_iota(jnp.int32, (tq, V7X_LANES), 1)
    low = lane < HEAD_DIM
    key_tile = ATTN_KEY_TILE

    slots = [(s, h) for s in range(n_seq) for h in range(N_HEADS)]
    qh = []
    for slot, (s, h) in enumerate(slots):
        qp = q_ref[s * tq:(s + 1) * tq, (h // 2) * V7X_LANES:(h // 2 + 1) * V7X_LANES]
        zero = jnp.zeros_like(qp)
        qh.append(jnp.where(low, zero, qp) if h % 2 else jnp.where(low, qp, zero))
        if has_sink:
            m_ref[slot] = jnp.full((1, tq), sink_ref[h] * LOG2_E, F32)
            acc_ref[slot] = jnp.where(acc_row < HEAD_DIM, 0.0, 1.0)
        else:
            m_ref[slot] = jnp.full((1, tq), NEG_INF, F32)
            acc_ref[slot] = jnp.zeros((acc_rows, tq), F32)

    def scores(item):
        slot, key_ref, _, keys, _ = item
        g = slots[slot][1] // group
        k = key_ref[keys, g * V7X_LANES:(g + 1) * V7X_LANES]
        return lax.dot_general(k, qh[slot], (((1,), (1,)), ((), ())), preferred_element_type=F32)

    def absorb(item, st):
        slot, _, val_ref, keys, mask = item
        g = slots[slot][1] // group
        vt = val_ref[keys, g * V7X_LANES:(g + 1) * V7X_LANES].T[:acc_rows]
        if mask is not None:
            st = jnp.where(mask, st, NEG_INF)
        m_old = m_ref[slot]
        m_new = jnp.maximum(m_old, jnp.max(st, axis=0, keepdims=True))
        alpha = jnp.exp2(m_old - m_new)
        p = jnp.exp2(st - m_new)
        acc_ref[slot] = alpha * acc_ref[slot] + _dot(vt, p.astype(BF16))
        m_ref[slot] = m_new

    def run(items):
        pending = {}
        for i in range(len(items) + ATTN_LOOKAHEAD):
            if i < len(items):
                pending[i] = scores(items[i])
            j = i - ATTN_LOOKAHEAD
            if j >= 0:
                absorb(items[j], pending.pop(j))

    def tile_items(key_ref, val_ref, keys_of_seq, mask):
        return [(slot, key_ref, val_ref, keys_of_seq(s), mask) for slot, (s, _) in enumerate(slots)]

    head_items = tile_items(ck_ref, cv_ref, lambda s: slice(None), None) if has_ctx else []
    if mode == "full":
        n_tiles = seq_len // key_tile
        if n_tiles == 1:
            run(head_items + tile_items(k_ref, v_ref, lambda s: slice(s * seq_len, (s + 1) * seq_len), None))
        else:
            run(head_items)
            per_step = ATTN_TILES_PER_STEP

            def body(t, carry):
                items = []
                for u in range(per_step):
                    keys = pl.ds(pl.multiple_of((t * per_step + u) * key_tile, key_tile), key_tile)
                    items += tile_items(k_ref, v_ref, lambda s: keys, None)
                run(items)
                return carry
            lax.fori_loop(0, n_tiles // per_step, body, 0)
    else:
        span = tq + 2 * WINDOW
        q0 = qi * tq
        start = pl.multiple_of(jnp.clip(q0 - WINDOW, 0, seq_len - span), WINDOW)
        items = head_items
        for u in range(span // key_tile):
            k_abs = start + u * key_tile + lax.broadcasted_iota(jnp.int32, (key_tile, tq), 0)
            q_abs = q0 + lax.broadcasted_iota(jnp.int32, (key_tile, tq), 1)
            keys = pl.ds(start + u * key_tile, key_tile)
            items = items + tile_items(k_ref, v_ref, lambda s: keys, jnp.abs(q_abs - k_abs) <= WINDOW)
        run(items)

    def normalised(slot):
        acc = acc_ref[slot]
        return (acc[:HEAD_DIM] * (1.0 / acc[HEAD_DIM:HEAD_DIM + 1])).T

    for s in range(n_seq):
        for pair in range(N_HEADS // 2):
            halves = [normalised(s * N_HEADS + 2 * pair), normalised(s * N_HEADS + 2 * pair + 1)]
            o_ref[s * tq:(s + 1) * tq, pair * V7X_LANES:(pair + 1) * V7X_LANES] = (
                jnp.concatenate(halves, axis=1).astype(o_ref.dtype))


def _attn_call(q, kd, vd, seq_len, mode, ctx=None, sink=None):
    ntok = q.shape[0]
    nb = ntok // seq_len
    tq = min(ATTN_Q_TILE, seq_len)
    nq = seq_len // tq
    n_seq = ATTN_SEQS_PER_STEP if nq == 1 else 1
    group = N_HEADS // N_KV
    in_specs = [
        pl.BlockSpec((n_seq * tq, D_Q), lambda b, i: (b * nq + i, 0)),
        pl.BlockSpec((n_seq * seq_len, 2 * D_KV), lambda b, i: (b, 0)),
        pl.BlockSpec((n_seq * seq_len, 2 * D_KV), lambda b, i: (b, 0)),
    ]
    args = [q, kd, vd]
    if ctx is not None:
        assert n_seq == 1
        past = ctx[0].shape[1]
        in_specs += [pl.BlockSpec((None, past, 2 * D_KV), lambda b, i: (b, 0, 0))] * 2
        args += list(ctx)
    if sink is not None:
        in_specs.append(pl.BlockSpec(memory_space=pltpu.SMEM))
        args.append(sink)
    return pl.pallas_call(
        functools.partial(_attn_kernel, tq=tq, seq_len=seq_len, n_seq=n_seq, mode=mode,
                          has_ctx=ctx is not None, has_sink=sink is not None),
        grid=(nb // n_seq, nq),
        in_specs=in_specs,
        out_specs=pl.BlockSpec((n_seq * tq, D_Q), lambda b, i: (b * nq + i, 0)),
        out_shape=jax.ShapeDtypeStruct((ntok, D_Q), BF16),
        scratch_shapes=[
            pltpu.VMEM((n_seq * N_HEADS, 1, tq), F32),
            pltpu.VMEM((n_seq * N_HEADS, HEAD_DIM + ATTN_ONES, tq), F32),
        ],
        compiler_params=_params("parallel", "parallel"),
        name="attn_" + mode + ("_sink" if sink is not None else ""),
    )(*args)


def _mix_out_kernel(oa_ref, ob_ref, oc_ref, sga_ref, sgb_ref, sgc_ref, x_ref, mod_ref,
                    woa_ref, wob_ref, woc_ref, wout_ref, o_ref):
    merged = (sga_ref[...].astype(F32) * _dot(oa_ref[...], woa_ref[...])
              + sgb_ref[...].astype(F32) * _dot(ob_ref[...], wob_ref[...])
              + sgc_ref[...].astype(F32) * _dot(oc_ref[...], woc_ref[...]))
    out = _dot(merged.astype(BF16), wout_ref[...])
    o_ref[...] = x_ref[...] + mod_ref[0][2:3] * out


def _mix_out_call(oa, ob, oc, sga, sgb, sgc, x, mod, w_oa, w_ob, w_oc, w_out, seq_len):
    ntok = x.shape[0]
    tm = TOKEN_TILE
    tiles_per_seq = max(seq_len // tm, 1)
    cond_of = (lambda i: i // tiles_per_seq) if mod.shape[0] > 1 else (lambda i: 0)
    row = lambda w: pl.BlockSpec((tm, w), lambda i: (i, 0))
    return pl.pallas_call(
        _mix_out_kernel,
        grid=(ntok // tm,),
        in_specs=[row(D_LRU), row(D_Q), row(D_Q), row(D_MODEL), row(D_MODEL), row(D_MODEL), row(D_MODEL),
                  pl.BlockSpec((1, 6, D_MODEL), lambda i: (cond_of(i), 0, 0)),
                  _resident((D_LRU, D_MODEL)), _resident((D_Q, D_MODEL)), _resident((D_Q, D_MODEL)),
                  _resident((D_MODEL, D_MODEL))],
        out_specs=row(D_MODEL),
        out_shape=jax.ShapeDtypeStruct((ntok, D_MODEL), F32),
        compiler_params=_params("parallel"),
        name="mix_out",
    )(oa, ob, oc, sga, sgb, sgc, x, mod, w_oa, w_ob, w_oc, w_out)


def _ffn_kernel(*refs, tm, seq_len, final):
    it = iter(refs)
    x_ref, prev_ref, next_ref, mod_ref, g_ref, wup_ref, cw_ref, cb_ref, wdn_ref = (next(it) for _ in range(9))
    fg_ref = next(it) if final else None
    o_ref = next(it)
    hext_ref, gate_ref, act_ref = next(it), next(it), next(it)

    halo = V7X_SUBLANES_F32
    mod = mod_ref[0]
    shift, scale, gate2 = mod[3:4], mod[4:5], mod[5:6]
    gain = g_ref[...] * (1.0 + scale)
    x = x_ref[...]

    def modulated(v):
        return (_rms_rows(v) * gain + shift).astype(BF16)

    hext_ref[0:halo, :] = modulated(prev_ref[...])
    hext_ref[halo:halo + tm, :] = modulated(x)
    hext_ref[halo + tm:, :] = modulated(next_ref[...])

    pos = (pl.program_id(0) * tm + lax.broadcasted_iota(jnp.int32, (tm, 1), 0)) % seq_len
    has_prev = (pos != 0).astype(F32)
    has_next = (pos != seq_len - 1).astype(F32)

    cw = cw_ref[...]
    for c in range(D_FF // FFN_CHUNK):
        cols = slice(c * FFN_CHUNK, (c + 1) * FFN_CHUNK)
        vcols = slice(D_FF + c * FFN_CHUNK, D_FF + (c + 1) * FFN_CHUNK)
        gate_ref[...] = _dot(hext_ref[...], wup_ref[:, cols])
        val = _dot(hext_ref[halo:halo + tm, :], wup_ref[:, vcols])
        gate = (cb_ref[:, cols]
                + cw[0:1, cols] * (gate_ref[halo - 1:halo - 1 + tm, :] * has_prev)
                + cw[1:2, cols] * gate_ref[halo:halo + tm, :]
                + cw[2:3, cols] * (gate_ref[halo + 1:halo + 1 + tm, :] * has_next))
        act_ref[:, cols] = (_gelu_tanh(gate) * val).astype(BF16)

    y = x + gate2 * _dot(act_ref[...], wdn_ref[...])
    if final:
        y = _rms_rows(y) * fg_ref[...]
    o_ref[...] = y


def _ffn_call(x, mod, g, w_up, conv_w, conv_b, w_down, seq_len, final_g=None):
    ntok = x.shape[0]
    tm = TOKEN_TILE
    halo = V7X_SUBLANES_F32
    tiles_per_seq = max(seq_len // tm, 1)
    cond_of = (lambda i: i // tiles_per_seq) if mod.shape[0] > 1 else (lambda i: 0)
    n_halo_blocks = ntok // halo
    in_specs = [
        pl.BlockSpec((tm, D_MODEL), lambda i: (i, 0)),
        pl.BlockSpec((halo, D_MODEL), lambda i: (jnp.maximum(i * (tm // halo) - 1, 0), 0)),
        pl.BlockSpec((halo, D_MODEL), lambda i: (jnp.minimum((i + 1) * (tm // halo), n_halo_blocks - 1), 0)),
        pl.BlockSpec((1, 6, D_MODEL), lambda i: (cond_of(i), 0, 0)),
        _resident((1, D_MODEL)),
        _resident((D_MODEL, 2 * D_FF)),
        _resident((3, D_FF)),
        _resident((1, D_FF)),
        _resident((D_FF, D_MODEL)),
    ]
    args = [x, x, x, mod, g, w_up, conv_w, conv_b, w_down]
    final = final_g is not None
    if final:
        in_specs.append(_resident((1, D_MODEL)))
        args.append(final_g)
    return pl.pallas_call(
        functools.partial(_ffn_kernel, tm=tm, seq_len=seq_len, final=final),
        grid=(ntok // tm,),
        in_specs=in_specs,
        out_specs=pl.BlockSpec((tm, D_MODEL), lambda i: (i, 0)),
        out_shape=jax.ShapeDtypeStruct((ntok, D_MODEL), F32),
        scratch_shapes=[
            pltpu.VMEM((tm + 2 * halo, D_MODEL), BF16),
            pltpu.VMEM((tm + 2 * halo, FFN_CHUNK), F32),
            pltpu.VMEM((tm, D_FF), BF16),
        ],
        compiler_params=_params("parallel"),
        name="ffn",
    )(*args)


def _rope_tables(seq_len):
    nf = HEAD_DIM // 4
    inv = ROPE_THETA ** (-jnp.arange(nf, dtype=F32) / nf)
    t = jnp.arange(seq_len)
    ang_row = (t // GRID_W).astype(F32)[:, None] * inv[None, :]
    ang_col = (t % GRID_W).astype(F32)[:, None] * inv[None, :]
    cos = jnp.concatenate([jnp.cos(ang_row)] * 2 + [jnp.cos(ang_col)] * 2, axis=1)
    sin = jnp.concatenate([-jnp.sin(ang_row), jnp.sin(ang_row), -jnp.sin(ang_col), jnp.sin(ang_col)], axis=1)
    reps = V7X_LANES // HEAD_DIM
    return jnp.concatenate([cos] * reps, axis=1), jnp.concatenate([sin] * reps, axis=1)


def _segment_mean_matrix(width):
    seg = np.arange(width) // HEAD_DIM
    return jnp.asarray((seg[:, None] == seg[None, :]).astype(np.float32) / HEAD_DIM, dtype=BF16)


def _dup_cache(c):
    return jnp.concatenate([c[:, :, 0], c[:, :, 0], c[:, :, 1], c[:, :, 1]], axis=-1).astype(BF16)


def _ext_cache(c):
    b, n = c.shape[:2]
    ones = jnp.ones((b, n, ATTN_ONES), c.dtype)
    zeros = jnp.zeros((b, n, V7X_LANES - HEAD_DIM - ATTN_ONES), c.dtype)
    return jnp.concatenate([c[:, :, 0], ones, zeros, c[:, :, 1], ones, zeros], axis=-1).astype(BF16)


def kernel(x_prompt, x_sample, c, cache_kb, cache_vb, cache_kc, cache_vc, state_lru, c_ctx, norm1_g, norm2_g, w_mod, b_mod, w_in, lru_conv_w, lru_conv_b, lru_wa, lru_ba, lru_wx, lru_bx, lru_lam, qnorm_g, knorm_g, sink_c, w_oa, w_ob, w_oc, w_out, w_up, ffn_conv_w, ffn_conv_b, w_down, final_g):
    batch, seq, d = x_prompt.shape
    dec_batch, dec_seq, _ = x_sample.shape
    depth = w_in.shape[0]
    assert d == D_MODEL and depth == DEPTH and dec_batch + 1 <= 8

    cond8 = jnp.zeros((8, d), F32).at[0].set(c_ctx).at[1:1 + dec_batch].set(c)
    mod_all = _mod_call(cond8, w_mod, b_mod).reshape(depth, 8, 6, d)

    w_in_b, w_oa_b, w_ob_b, w_oc_b = (w.astype(BF16) for w in (w_in, w_oa, w_ob, w_oc))
    w_out_b, w_up_b, w_down_b = (w.astype(BF16) for w in (w_out, w_up, w_down))
    w_gate = jnp.concatenate([lru_wa, lru_wx], axis=-1).astype(BF16)
    seg_q, seg_k = _segment_mean_matrix(D_Q), _segment_mean_matrix(D_KV)
    rope_tabs = _rope_tables(dec_seq)
    qg = jnp.tile(qnorm_g, (1, N_HEADS))
    kg = jnp.tile(knorm_g, (1, N_KV))

    def layer(l, x, mod, seq_len, h0, ctx, is_last):
        is_ctx = ctx is None
        outs = _in_proj_call(x, mod, norm1_g[l][None], w_in_b[l], qg[l][None], kg[l][None], seg_q, seg_k,
                             None if is_ctx else rope_tabs, seq_len, emit_kv=is_ctx)
        xa, gya, qb, kbd, vbd, qc, kcd, vcd, sga, sgb, sgc = outs[:11]
        lru_args = lambda dr: (lru_conv_w[l], lru_conv_b[l][None], w_gate[l, dr], lru_ba[l, dr][None],
                               lru_bx[l, dr][None], lru_lam[l, dr][None], h0[:, dr][:, None, :], seq_len)
        hf, last_f = _lru_call(xa, *lru_args(0), reverse=False)
        oa, last_b = _lru_call(xa, *lru_args(1), reverse=True, hf=hf, gya=gya)
        if is_ctx:
            ob = _attn_call(qb, kbd, vbd, seq_len, "full")
            oc = _attn_call(qc, kcd, vcd, seq_len, "full", sink=sink_c[l])
        else:
            ob = _attn_call(qb, kbd, vbd, seq_len, "full", ctx=(ctx["kb"], ctx["vb"]))
            oc = _attn_call(qc, kcd, vcd, seq_len, "window", ctx=(ctx["kc"], ctx["vc"]), sink=sink_c[l])
        x = _mix_out_call(oa, ob, oc, sga, sgb, sgc, x, mod, w_oa_b[l], w_ob_b[l], w_oc_b[l], w_out_b[l], seq_len)
        x = _ffn_call(x, mod, norm2_g[l][None], w_up_b[l], ffn_conv_w[l], ffn_conv_b[l][None], w_down_b[l],
                      seq_len, final_g=final_g[None] if is_last else None)
        return x, outs[11:], (last_f, last_b)

    xp = x_prompt.reshape(batch * seq, d)
    xs = x_sample.reshape(dec_batch * dec_seq, d)
    zeros_h0 = jnp.zeros((batch, 2, D_LRU), F32)
    kbs, vbs, kcs, vcs, lrus = [], [], [], [], []
    for l in range(depth):
        is_last = l == depth - 1
        xp, (kb, vb, kc, vc), (last_f, last_b) = layer(l, xp, mod_all[l, 0:1], seq, zeros_h0, None, is_last)
        kbs.append(kb)
        vbs.append(vb)
        kcs.append(kc)
        vcs.append(vc)
        lrus.append(jnp.concatenate([last_f, last_b], axis=1))
        cached = {"kb": _dup_cache(cache_kb[:, l]), "vb": _ext_cache(cache_vb[:, l]),
                  "kc": _dup_cache(cache_kc[:, l]), "vc": _ext_cache(cache_vc[:, l])}
        xs, _, _ = layer(l, xs, mod_all[l, 1:1 + dec_batch], dec_seq, state_lru[:, l], cached, is_last)

    def stack_kv(parts):
        return jnp.stack([p.reshape(batch, seq, N_KV, HEAD_DIM) for p in parts], axis=1)

    y_prompt = xp.reshape(batch, seq, d)
    y_sample = xs.reshape(dec_batch, dec_seq, d)
    return (y_prompt, y_sample, stack_kv(kbs), stack_kv(vbs), stack_kv(kcs), stack_kv(vcs),
            jnp.stack(lrus, axis=1))
```

```python
import functools

import jax
import jax.numpy as jnp
import numpy as np
from jax import lax
from jax.experimental import pallas as pl
from jax.experimental.pallas import tpu as pltpu

D_MODEL = 1024
DEPTH = 2
GRID_W = 64
D_LRU = 1024
LRU_BLOCKS = 8
LRU_BLOCK = D_LRU // LRU_BLOCKS
LRU_C = 8.0
HEAD_DIM = 64
N_HEADS = 8
N_KV = 2
D_Q = N_HEADS * HEAD_DIM
D_KV = N_KV * HEAD_DIM
WINDOW = 128
D_FF = 2816
ROPE_THETA = 10000.0
NORM_EPS = 1e-6
NEG_INF = -1e30
ATTN_SCALE = HEAD_DIM ** -0.5
LOG2_E = 1.4426950408889634
LN_2 = 0.6931471805599453
IN_SIZES = (D_LRU, D_LRU, D_Q, D_KV, D_KV, D_Q, D_KV, D_KV, D_MODEL, D_MODEL, D_MODEL)
IN_OFF = tuple(int(v) for v in np.cumsum((0,) + IN_SIZES))
D_IN = IN_OFF[-1]

V7X_LANES = 128
V7X_SUBLANES_F32 = 8
V7X_SUBLANES_BF16 = 16
V7X_VMEM_LIMIT_BYTES = 56 * 1024 * 1024

TOKEN_TILE = 512
FFN_CHUNK = 256
ATTN_Q_TILE = 256
ATTN_KEY_TILE = 256
ATTN_SEQS_PER_STEP = 4
ATTN_TILES_PER_STEP = 4
ATTN_LOOKAHEAD = 5
ATTN_ONES = 16
LRU_CHUNK = 512
LRU_SEQS_PER_STEP = 4
MOD_COL_TILE = 1536

F32 = jnp.float32
BF16 = jnp.bfloat16


def _params(*sem):
    return pltpu.CompilerParams(dimension_semantics=sem, vmem_limit_bytes=V7X_VMEM_LIMIT_BYTES)


def _resident(shape, lead=()):
    nd = len(shape)
    lead = tuple(int(i) for i in lead)
    return pl.BlockSpec((None,) * len(lead) + tuple(shape), lambda *_: lead + (0,) * nd,
                        pipeline_mode=pl.Buffered(1))


def _sigmoid(x):
    return 1.0 / (1.0 + jnp.exp(-x))


def _gelu_tanh(x):
    return 0.5 * x * (1.0 + jnp.tanh(0.7978845608028654 * (x + 0.044715 * (x * x * x))))


def _rms_rows(x):
    return x * lax.rsqrt(jnp.mean(x * x, axis=-1, keepdims=True) + NORM_EPS)


def _dot(a, b):
    return jnp.dot(a, b, preferred_element_type=F32)


def _mod_kernel(c_ref, w_ref, b_ref, o_ref):
    c = c_ref[...]
    s = (c * _sigmoid(c)).astype(BF16)
    o_ref[0] = _dot(s, w_ref[0].astype(BF16)) + b_ref[0]


def _mod_call(cond8, w_mod, b_mod):
    depth, d, n = w_mod.shape
    tn = MOD_COL_TILE
    return pl.pallas_call(
        _mod_kernel,
        grid=(depth, n // tn),
        in_specs=[
            pl.BlockSpec((8, d), lambda l, j: (0, 0)),
            pl.BlockSpec((1, d, tn), lambda l, j: (l, 0, j)),
            pl.BlockSpec((1, 1, tn), lambda l, j: (l, 0, j)),
        ],
        out_specs=pl.BlockSpec((1, 8, tn), lambda l, j: (l, 0, j)),
        out_shape=jax.ShapeDtypeStruct((depth, 8, n), F32),
        compiler_params=_params("parallel", "parallel"),
        name="mod",
    )(cond8, w_mod, b_mod.reshape(depth, 1, n))


def _head_rms(x, seg_ref, g):
    sq = x * x
    hi = sq.astype(BF16)
    lo = (sq - hi.astype(F32)).astype(BF16)
    ms = _dot(hi, seg_ref[...]) + _dot(lo, seg_ref[...])
    return x * lax.rsqrt(ms + NORM_EPS) * g


def _rope(x, cos, sin):
    w = x.shape[1]
    lane = lax.broadcasted_iota(jnp.int32, x.shape, 1)
    fwd = pltpu.roll(x, w - 16, 1)
    bwd = pltpu.roll(x, 16, 1)
    partner = jnp.where((lane % 32) < 16, fwd, bwd)
    return x * cos + partner * sin


def _dup_heads(x):
    lane = lax.broadcasted_iota(jnp.int32, x.shape, 1)
    swapped = pltpu.roll(x, HEAD_DIM, 1)
    low = lane < HEAD_DIM
    return jnp.concatenate([jnp.where(low, x, swapped), jnp.where(low, swapped, x)], axis=1)


def _ext_values(x):
    lane = lax.broadcasted_iota(jnp.int32, x.shape, 1)
    swapped = pltpu.roll(x, HEAD_DIM, 1)
    low = lane < HEAD_DIM
    tail = jnp.where(lane < HEAD_DIM + ATTN_ONES, 1.0, 0.0)
    return jnp.concatenate([jnp.where(low, x, tail), jnp.where(low, swapped, tail)], axis=1)


def _in_proj_kernel(*refs, rope, emit_kv):
    it = iter(refs)
    x_ref, mod_ref, g_ref, w_ref, qg_ref, kg_ref, seg_q_ref, seg_k_ref = (next(it) for _ in range(8))
    cos_ref = sin_ref = None
    if rope:
        cos_ref, sin_ref = next(it), next(it)
    (xa_ref, gya_ref, qb_ref, kbd_ref, vbd_ref, qc_ref, kcd_ref, vcd_ref,
     sga_ref, sgb_ref, sgc_ref) = (next(it) for _ in range(11))
    if emit_kv:
        kb_ref, vb_ref, kc_ref, vc_ref = (next(it) for _ in range(4))

    mod = mod_ref[0]
    shift, scale = mod[0:1], mod[1:2]
    h = (_rms_rows(x_ref[...]) * (g_ref[...] * (1.0 + scale)) + shift).astype(BF16)

    def seg(lo, hi):
        return _dot(h, w_ref[:, IN_OFF[lo]:IN_OFF[hi]])

    xa_ref[...] = seg(0, 1).astype(BF16)
    gya_ref[...] = _gelu_tanh(seg(1, 2)).astype(BF16)
    sga_ref[...] = _sigmoid(seg(8, 9)).astype(BF16)
    sgb_ref[...] = _sigmoid(seg(9, 10)).astype(BF16)
    sgc_ref[...] = _sigmoid(seg(10, 11)).astype(BF16)

    qb = _head_rms(seg(2, 3), seg_q_ref, qg_ref[...])
    kv_b = seg(3, 5)
    kb = _head_rms(kv_b[:, :D_KV], seg_k_ref, kg_ref[...])
    vb = kv_b[:, D_KV:]
    qkv_c = seg(5, 8)
    qc = qkv_c[:, :D_Q]
    kc = qkv_c[:, D_Q:D_Q + D_KV]
    vc = qkv_c[:, D_Q + D_KV:]
    if emit_kv:
        kb_ref[...] = kb
        vb_ref[...] = vb
        kc_ref[...] = kc
        vc_ref[...] = vc
    if rope:
        cos, sin = cos_ref[...], sin_ref[...]
        cos_q = jnp.concatenate([cos] * (D_Q // V7X_LANES), axis=1)
        sin_q = jnp.concatenate([sin] * (D_Q // V7X_LANES), axis=1)
        qb = _rope(qb, cos_q, sin_q)
        qc = _rope(qc, cos_q, sin_q)
        kb = _rope(kb, cos, sin)
        kc = _rope(kc, cos, sin)
    qb_ref[...] = (qb * (ATTN_SCALE * LOG2_E)).astype(BF16)
    qc_ref[...] = (qc * (ATTN_SCALE * LOG2_E)).astype(BF16)
    kbd_ref[...] = _dup_heads(kb).astype(BF16)
    vbd_ref[...] = _ext_values(vb).astype(BF16)
    kcd_ref[...] = _dup_heads(kc).astype(BF16)
    vcd_ref[...] = _ext_values(vc).astype(BF16)


def _in_proj_call(layer, x, mod, g, w_in, qg, kg, seg_q, seg_k, rope_tabs, seq_len, emit_kv):
    ntok = x.shape[0]
    tm = TOKEN_TILE
    tiles_per_seq = max(seq_len // tm, 1)
    ncond = mod.shape[0]
    cond_of = (lambda i: i // tiles_per_seq) if ncond > 1 else (lambda i: 0)
    row = lambda w: pl.BlockSpec((tm, w), lambda i: (i, 0))
    in_specs = [
        row(D_MODEL),
        pl.BlockSpec((1, 6, D_MODEL), lambda i: (cond_of(i), 0, 0)),
        _resident((1, D_MODEL)),
        _resident((D_MODEL, D_IN), lead=(layer,)),
        _resident((1, D_Q)),
        _resident((1, D_KV)),
        _resident((D_Q, D_Q)),
        _resident((D_KV, D_KV)),
    ]
    args = [x, mod, g, w_in, qg, kg, seg_q, seg_k]
    rope = rope_tabs is not None
    if rope:
        in_specs += [pl.BlockSpec((tm, V7X_LANES), lambda i: (i % tiles_per_seq, 0))] * 2
        args += list(rope_tabs)
    widths = [D_LRU, D_LRU, D_Q, 2 * D_KV, 2 * D_KV, D_Q, 2 * D_KV, 2 * D_KV, D_MODEL, D_MODEL, D_MODEL]
    out_specs = [row(w) for w in widths]
    out_shape = [jax.ShapeDtypeStruct((ntok, w), BF16) for w in widths]
    if emit_kv:
        out_specs += [row(D_KV)] * 4
        out_shape += [jax.ShapeDtypeStruct((ntok, D_KV), F32)] * 4
    return pl.pallas_call(
        functools.partial(_in_proj_kernel, rope=rope, emit_kv=emit_kv),
        grid=(ntok // tm,),
        in_specs=in_specs,
        out_specs=out_specs,
        out_shape=out_shape,
        compiler_params=_params("parallel"),
        name="in_proj",
    )(*args)


def _lru_kernel(*refs, chunk, n_chunks, n_seq, reverse):
    it = iter(refs)
    (cur_ref, prev_ref, next_ref, cw_ref, cb_ref, wg_ref, ba_ref, bx_ref, lam_ref,
     h0_ref) = (next(it) for _ in range(10))
    if reverse:
        hf_ref, gya_ref = next(it), next(it)
    out_ref, last_ref = next(it), next(it)
    a_ref, u_ref, state_ref = next(it), next(it), next(it)
    hs_ref = next(it) if reverse else out_ref

    j = pl.program_id(1)
    jj = (n_chunks - 1 - j) if reverse else j
    halo = V7X_SUBLANES_F32
    n_ext = chunk + 2 * halo
    cw, cb = cw_ref[...], cb_ref[...]
    lam = lam_ref[...]
    softplus_neg_lam = jnp.maximum(-lam, 0.0) + jnp.log1p(jnp.exp(-jnp.abs(lam)))
    slope = (-0.5 * LRU_C * LOG2_E) * softplus_neg_lam

    for s in range(n_seq):
        prev = prev_ref[s].astype(F32)[V7X_SUBLANES_BF16 - halo:]
        nxt = next_ref[s].astype(F32)[:halo]
        ext = jnp.concatenate([jnp.where(jj > 0, prev, 0.0), cur_ref[s].astype(F32),
                               jnp.where(jj < n_chunks - 1, nxt, 0.0)], axis=0)
        taps = (pltpu.roll(ext, 2, 0), pltpu.roll(ext, 1, 0), ext, pltpu.roll(ext, n_ext - 1, 0))
        xc = cb + cw[0:1] * taps[0][halo:halo + chunk]
        for k in range(1, 4):
            xc = xc + cw[k:k + 1] * taps[k][halo:halo + chunk]
        xc_b = xc.astype(BF16)
        half_xc = 0.5 * xc

        for n in range(LRU_BLOCKS):
            cols = slice(n * LRU_BLOCK, (n + 1) * LRU_BLOCK)
            z = _dot(xc_b[:, cols], wg_ref[n])
            t_r = jnp.tanh(z[:, :LRU_BLOCK] + ba_ref[:, cols])
            t_i = jnp.tanh(z[:, LRU_BLOCK:] + bx_ref[:, cols])
            log2_a = slope[:, cols] + slope[:, cols] * t_r
            a = jnp.exp2(log2_a)
            one_minus_a2 = jnp.tanh(log2_a * (-LN_2)) * (a * a + 1.0)
            root = jnp.where(one_minus_a2 > 0.0, one_minus_a2 * lax.rsqrt(one_minus_a2), 0.0)
            hx = half_xc[:, cols]
            a_ref[s, :, cols] = a
            u_ref[s, :, cols] = root * (hx + hx * t_i)

    @pl.when(j == 0)
    def _():
        state_ref[...] = h0_ref[...]

    def step(t, hs):
        idx = (chunk - 1 - t) if reverse else t
        new = []
        for s in range(n_seq):
            h = a_ref[s, pl.ds(idx, 1), :] * hs[s] + u_ref[s, pl.ds(idx, 1), :]
            hs_ref[s, pl.ds(idx, 1), :] = h
            new.append(h)
        return tuple(new)

    hs = lax.fori_loop(0, chunk, step, tuple(state_ref[s] for s in range(n_seq)), unroll=8)
    for s in range(n_seq):
        state_ref[s] = hs[s]

    if reverse:
        out_ref[...] = ((hf_ref[...] + hs_ref[...]) * gya_ref[...].astype(F32)).astype(out_ref.dtype)

    @pl.when(j == n_chunks - 1)
    def _():
        for s in range(n_seq):
            last_ref[s] = hs[s]


def _lru_call(layer, xa, conv_w, conv_b, w_gate, ba, bx, lam, h0, seq_len, reverse, hf=None, gya=None):
    nb = xa.shape[0]
    chunk = min(LRU_CHUNK, seq_len)
    nt = seq_len // chunk
    n_seq = min(LRU_SEQS_PER_STEP, nb)
    hb = V7X_SUBLANES_BF16
    pos = (lambda j: nt - 1 - j) if reverse else (lambda j: j)
    cur = lambda b, j: (b, pos(j), 0)
    prev = lambda b, j: (b, jnp.maximum(pos(j) * (chunk // hb) - 1, 0), 0)
    nxt = lambda b, j: (b, jnp.minimum((pos(j) + 1) * (chunk // hb), seq_len // hb - 1), 0)
    per_seq = pl.BlockSpec((n_seq, 1, D_LRU), lambda b, j: (b, 0, 0))
    in_specs = [
        pl.BlockSpec((n_seq, chunk, D_LRU), cur),
        pl.BlockSpec((n_seq, hb, D_LRU), prev),
        pl.BlockSpec((n_seq, hb, D_LRU), nxt),
        _resident((4, D_LRU)),
        _resident((1, D_LRU)),
        _resident((LRU_BLOCKS, LRU_BLOCK, 2 * LRU_BLOCK), lead=(layer, int(reverse))),
        _resident((1, D_LRU)),
        _resident((1, D_LRU)),
        _resident((1, D_LRU)),
        per_seq,
    ]
    args = [xa, xa, xa, conv_w, conv_b, w_gate, ba, bx, lam, h0]
    scratch = [
        pltpu.VMEM((n_seq, chunk, D_LRU), F32),
        pltpu.VMEM((n_seq, chunk, D_LRU), F32),
        pltpu.VMEM((n_seq, 1, D_LRU), F32),
    ]
    if reverse:
        in_specs += [pl.BlockSpec((n_seq, chunk, D_LRU), cur)] * 2
        args += [hf, gya]
        scratch.append(pltpu.VMEM((n_seq, chunk, D_LRU), F32))
    return pl.pallas_call(
        functools.partial(_lru_kernel, chunk=chunk, n_chunks=nt, n_seq=n_seq, reverse=reverse),
        grid=(nb // n_seq, nt),
        in_specs=in_specs,
        out_specs=[pl.BlockSpec((n_seq, chunk, D_LRU), cur), per_seq],
        out_shape=[jax.ShapeDtypeStruct((nb, seq_len, D_LRU), BF16 if reverse else F32),
                   jax.ShapeDtypeStruct((nb, 1, D_LRU), F32)],
        scratch_shapes=scratch,
        compiler_params=_params("arbitrary", "arbitrary"),
        name="lru_bwd" if reverse else "lru_fwd",
    )(*args)


def _attn_kernel(*refs, tq, seq_len, n_seq, mode, has_ctx, has_sink):
    it = iter(refs)
    q_ref, k_ref, v_ref = next(it), next(it), next(it)
    ck_ref = cv_ref = sink_ref = None
    if has_ctx:
        ck_ref, cv_ref = next(it), next(it)
    if has_sink:
        sink_ref = next(it)
    o_ref = next(it)
    m_ref, acc_ref = next(it), next(it)

    qi = pl.program_id(1)
    group = N_HEADS // N_KV
    acc_rows = HEAD_DIM + ATTN_ONES
    acc_row = lax.broadcasted_iota(jnp.int32, (acc_rows, tq), 0)
    lane = lax.broadcasted_iota(jnp.int32, (tq, V7X_LANES), 1)
    low = lane < HEAD_DIM
    key_tile = ATTN_KEY_TILE

    slots = [(s, h) for s in range(n_seq) for h in range(N_HEADS)]
    qh = []
    for slot, (s, h) in enumerate(slots):
        qp = q_ref[s * tq:(s + 1) * tq, (h // 2) * V7X_LANES:(h // 2 + 1) * V7X_LANES]
        zero = jnp.zeros_like(qp)
        qh.append(jnp.where(low, zero, qp) if h % 2 else jnp.where(low, qp, zero))
        if has_sink:
            m_ref[slot] = jnp.full((1, tq), sink_ref[h] * LOG2_E, F32)
            acc_ref[slot] = jnp.where(acc_row < HEAD_DIM, 0.0, 1.0)
        else:
            m_ref[slot] = jnp.full((1, tq), NEG_INF, F32)
            acc_ref[slot] = jnp.zeros((acc_rows, tq), F32)

    def scores(item):
        slot, key_ref, _, keys, _ = item
        g = slots[slot][1] // group
        k = key_ref[keys, g * V7X_LANES:(g + 1) * V7X_LANES]
        return lax.dot_general(k, qh[slot], (((1,), (1,)), ((), ())), preferred_element_type=F32)

    def absorb(item, st):
        slot, _, val_ref, keys, mask = item
        g = slots[slot][1] // group
        vt = val_ref[keys, g * V7X_LANES:(g + 1) * V7X_LANES].T[:acc_rows]
        if mask is not None:
            st = jnp.where(mask, st, NEG_INF)
        m_old = m_ref[slot]
        m_new = jnp.maximum(m_old, jnp.max(st, axis=0, keepdims=True))
        alpha = jnp.exp2(m_old - m_new)
        p = jnp.exp2(st - m_new)
        acc_ref[slot] = alpha * acc_ref[slot] + _dot(vt, p.astype(BF16))
        m_ref[slot] = m_new

    def run(items):
        pending = {}
        for i in range(len(items) + ATTN_LOOKAHEAD):
            if i < len(items):
                pending[i] = scores(items[i])
            j = i - ATTN_LOOKAHEAD
            if j >= 0:
                absorb(items[j], pending.pop(j))

    def tile_items(key_ref, val_ref, keys_of_seq, mask):
        return [(slot, key_ref, val_ref, keys_of_seq(s), mask) for slot, (s, _) in enumerate(slots)]

    head_items = tile_items(ck_ref, cv_ref, lambda s: slice(None), None) if has_ctx else []
    if mode == "full":
        n_tiles = seq_len // key_tile
        if n_tiles == 1:
            run(head_items + tile_items(k_ref, v_ref, lambda s: slice(s * seq_len, (s + 1) * seq_len), None))
        else:
            run(head_items)
            per_step = ATTN_TILES_PER_STEP

            def body(t, carry):
                items = []
                for u in range(per_step):
                    keys = pl.ds(pl.multiple_of((t * per_step + u) * key_tile, key_tile), key_tile)
                    items += tile_items(k_ref, v_ref, lambda s: keys, None)
                run(items)
                return carry
            lax.fori_loop(0, n_tiles // per_step, body, 0)
    else:
        span = tq + 2 * WINDOW
        q0 = qi * tq
        start = pl.multiple_of(jnp.clip(q0 - WINDOW, 0, seq_len - span), WINDOW)
        items = head_items
        for u in range(span // key_tile):
            k_abs = start + u * key_tile + lax.broadcasted_iota(jnp.int32, (key_tile, tq), 0)
            q_abs = q0 + lax.broadcasted_iota(jnp.int32, (key_tile, tq), 1)
            keys = pl.ds(start + u * key_tile, key_tile)
            items = items + tile_items(k_ref, v_ref, lambda s: keys, jnp.abs(q_abs - k_abs) <= WINDOW)
        run(items)

    def normalised(slot):
        acc = acc_ref[slot]
        return (acc[:HEAD_DIM] * (1.0 / acc[HEAD_DIM:HEAD_DIM + 1])).T

    for s in range(n_seq):
        for pair in range(N_HEADS // 2):
            halves = [normalised(s * N_HEADS + 2 * pair), normalised(s * N_HEADS + 2 * pair + 1)]
            o_ref[s * tq:(s + 1) * tq, pair * V7X_LANES:(pair + 1) * V7X_LANES] = (
                jnp.concatenate(halves, axis=1).astype(o_ref.dtype))


def _attn_call(q, kd, vd, seq_len, mode, ctx=None, sink=None):
    ntok = q.shape[0]
    nb = ntok // seq_len
    tq = min(ATTN_Q_TILE, seq_len)
    nq = seq_len // tq
    n_seq = ATTN_SEQS_PER_STEP if nq == 1 else 1
    group = N_HEADS // N_KV
    in_specs = [
        pl.BlockSpec((n_seq * tq, D_Q), lambda b, i: (b * nq + i, 0)),
        pl.BlockSpec((n_seq * seq_len, 2 * D_KV), lambda b, i: (b, 0)),
        pl.BlockSpec((n_seq * seq_len, 2 * D_KV), lambda b, i: (b, 0)),
    ]
    args = [q, kd, vd]
    if ctx is not None:
        assert n_seq == 1
        past = ctx[0].shape[1]
        in_specs += [pl.BlockSpec((None, past, 2 * D_KV), lambda b, i: (b, 0, 0))] * 2
        args += list(ctx)
    if sink is not None:
        in_specs.append(pl.BlockSpec(memory_space=pltpu.SMEM))
        args.append(sink)
    return pl.pallas_call(
        functools.partial(_attn_kernel, tq=tq, seq_len=seq_len, n_seq=n_seq, mode=mode,
                          has_ctx=ctx is not None, has_sink=sink is not None),
        grid=(nb // n_seq, nq),
        in_specs=in_specs,
        out_specs=pl.BlockSpec((n_seq * tq, D_Q), lambda b, i: (b * nq + i, 0)),
        out_shape=jax.ShapeDtypeStruct((ntok, D_Q), BF16),
        scratch_shapes=[
            pltpu.VMEM((n_seq * N_HEADS, 1, tq), F32),
            pltpu.VMEM((n_seq * N_HEADS, HEAD_DIM + ATTN_ONES, tq), F32),
        ],
        compiler_params=_params("parallel", "parallel"),
        name="attn_" + mode + ("_sink" if sink is not None else ""),
    )(*args)


def _mix_out_kernel(oa_ref, ob_ref, oc_ref, sga_ref, sgb_ref, sgc_ref, x_ref, mod_ref,
                    woa_ref, wob_ref, woc_ref, wout_ref, o_ref):
    merged = (sga_ref[...].astype(F32) * _dot(oa_ref[...], woa_ref[...])
              + sgb_ref[...].astype(F32) * _dot(ob_ref[...], wob_ref[...])
              + sgc_ref[...].astype(F32) * _dot(oc_ref[...], woc_ref[...]))
    out = _dot(merged.astype(BF16), wout_ref[...])
    o_ref[...] = x_ref[...] + mod_ref[0][2:3] * out


def _mix_out_call(layer, oa, ob, oc, sga, sgb, sgc, x, mod, w_oa, w_ob, w_oc, w_out, seq_len):
    ntok = x.shape[0]
    tm = TOKEN_TILE
    tiles_per_seq = max(seq_len // tm, 1)
    cond_of = (lambda i: i // tiles_per_seq) if mod.shape[0] > 1 else (lambda i: 0)
    row = lambda w: pl.BlockSpec((tm, w), lambda i: (i, 0))
    return pl.pallas_call(
        _mix_out_kernel,
        grid=(ntok // tm,),
        in_specs=[row(D_LRU), row(D_Q), row(D_Q), row(D_MODEL), row(D_MODEL), row(D_MODEL), row(D_MODEL),
                  pl.BlockSpec((1, 6, D_MODEL), lambda i: (cond_of(i), 0, 0)),
                  _resident((D_LRU, D_MODEL), lead=(layer,)), _resident((D_Q, D_MODEL), lead=(layer,)),
                  _resident((D_Q, D_MODEL), lead=(layer,)), _resident((D_MODEL, D_MODEL), lead=(layer,))],
        out_specs=row(D_MODEL),
        out_shape=jax.ShapeDtypeStruct((ntok, D_MODEL), F32),
        compiler_params=_params("parallel"),
        name="mix_out",
    )(oa, ob, oc, sga, sgb, sgc, x, mod, w_oa, w_ob, w_oc, w_out)


def _ffn_kernel(*refs, tm, seq_len, final):
    it = iter(refs)
    x_ref, prev_ref, next_ref, mod_ref, g_ref, wup_ref, cw_ref, cb_ref, wdn_ref = (next(it) for _ in range(9))
    fg_ref = next(it) if final else None
    o_ref = next(it)
    hext_ref, gate_ref, act_ref = next(it), next(it), next(it)

    halo = V7X_SUBLANES_F32
    mod = mod_ref[0]
    shift, scale, gate2 = mod[3:4], mod[4:5], mod[5:6]
    gain = g_ref[...] * (1.0 + scale)
    x = x_ref[...]

    def modulated(v):
        return (_rms_rows(v) * gain + shift).astype(BF16)

    hext_ref[0:halo, :] = modulated(prev_ref[...])
    hext_ref[halo:halo + tm, :] = modulated(x)
    hext_ref[halo + tm:, :] = modulated(next_ref[...])

    pos = (pl.program_id(0) * tm + lax.broadcasted_iota(jnp.int32, (tm, 1), 0)) % seq_len
    has_prev = (pos != 0).astype(F32)
    has_next = (pos != seq_len - 1).astype(F32)

    cw = cw_ref[...]
    for c in range(D_FF // FFN_CHUNK):
        cols = slice(c * FFN_CHUNK, (c + 1) * FFN_CHUNK)
        vcols = slice(D_FF + c * FFN_CHUNK, D_FF + (c + 1) * FFN_CHUNK)
        gate_ref[...] = _dot(hext_ref[...], wup_ref[:, cols])
        val = _dot(hext_ref[halo:halo + tm, :], wup_ref[:, vcols])
        gate = (cb_ref[:, cols]
                + cw[0:1, cols] * (gate_ref[halo - 1:halo - 1 + tm, :] * has_prev)
                + cw[1:2, cols] * gate_ref[halo:halo + tm, :]
                + cw[2:3, cols] * (gate_ref[halo + 1:halo + 1 + tm, :] * has_next))
        act_ref[:, cols] = (_gelu_tanh(gate) * val).astype(BF16)

    y = x + gate2 * _dot(act_ref[...], wdn_ref[...])
    if final:
        y = _rms_rows(y) * fg_ref[...]
    o_ref[...] = y


def _ffn_call(layer, x, mod, g, w_up, conv_w, conv_b, w_down, seq_len, final_g=None):
    ntok = x.shape[0]
    tm = TOKEN_TILE
    halo = V7X_SUBLANES_F32
    tiles_per_seq = max(seq_len // tm, 1)
    cond_of = (lambda i: i // tiles_per_seq) if mod.shape[0] > 1 else (lambda i: 0)
    n_halo_blocks = ntok // halo
    in_specs = [
        pl.BlockSpec((tm, D_MODEL), lambda i: (i, 0)),
        pl.BlockSpec((halo, D_MODEL), lambda i: (jnp.maximum(i * (tm // halo) - 1, 0), 0)),
        pl.BlockSpec((halo, D_MODEL), lambda i: (jnp.minimum((i + 1) * (tm // halo), n_halo_blocks - 1), 0)),
        pl.BlockSpec((1, 6, D_MODEL), lambda i: (cond_of(i), 0, 0)),
        _resident((1, D_MODEL)),
        _resident((D_MODEL, 2 * D_FF), lead=(layer,)),
        _resident((3, D_FF)),
        _resident((1, D_FF)),
        _resident((D_FF, D_MODEL), lead=(layer,)),
    ]
    args = [x, x, x, mod, g, w_up, conv_w, conv_b, w_down]
    final = final_g is not None
    if final:
        in_specs.append(_resident((1, D_MODEL)))
        args.append(final_g)
    return pl.pallas_call(
        functools.partial(_ffn_kernel, tm=tm, seq_len=seq_len, final=final),
        grid=(ntok // tm,),
        in_specs=in_specs,
        out_specs=pl.BlockSpec((tm, D_MODEL), lambda i: (i, 0)),
        out_shape=jax.ShapeDtypeStruct((ntok, D_MODEL), F32),
        scratch_shapes=[
            pltpu.VMEM((tm + 2 * halo, D_MODEL), BF16),
            pltpu.VMEM((tm + 2 * halo, FFN_CHUNK), F32),
            pltpu.VMEM((tm, D_FF), BF16),
        ],
        compiler_params=_params("parallel"),
        name="ffn",
    )(*args)


def _rope_tables(seq_len):
    nf = HEAD_DIM // 4
    n_rows = seq_len // GRID_W
    inv = ROPE_THETA ** (-jnp.arange(nf, dtype=F32) / nf)
    ang_r = jnp.arange(n_rows, dtype=F32)[:, None] * inv[None, :]
    ang_c = jnp.arange(GRID_W, dtype=F32)[:, None] * inv[None, :]
    by_row = lambda a: jnp.repeat(a, GRID_W, axis=0)
    by_col = lambda a: jnp.tile(a, (n_rows, 1))
    cos_r, sin_r, cos_c, sin_c = by_row(jnp.cos(ang_r)), by_row(jnp.sin(ang_r)), by_col(jnp.cos(ang_c)), by_col(jnp.sin(ang_c))
    cos = jnp.concatenate([cos_r, cos_r, cos_c, cos_c], axis=1)
    sin = jnp.concatenate([-sin_r, sin_r, -sin_c, sin_c], axis=1)
    reps = V7X_LANES // HEAD_DIM
    return jnp.concatenate([cos] * reps, axis=1), jnp.concatenate([sin] * reps, axis=1)


def _segment_mean_matrix(width):
    seg = np.arange(width) // HEAD_DIM
    return jnp.asarray((seg[:, None] == seg[None, :]).astype(np.float32) / HEAD_DIM, dtype=BF16)


def _dup_cache(c):
    return jnp.concatenate([c[:, :, 0], c[:, :, 0], c[:, :, 1], c[:, :, 1]], axis=-1).astype(BF16)


def _ext_cache(c):
    b, n = c.shape[:2]
    ones = jnp.ones((b, n, ATTN_ONES), c.dtype)
    zeros = jnp.zeros((b, n, V7X_LANES - HEAD_DIM - ATTN_ONES), c.dtype)
    return jnp.concatenate([c[:, :, 0], ones, zeros, c[:, :, 1], ones, zeros], axis=-1).astype(BF16)


def kernel(x_prompt, x_sample, c, cache_kb, cache_vb, cache_kc, cache_vc, state_lru, c_ctx, norm1_g, norm2_g, w_mod, b_mod, w_in, lru_conv_w, lru_conv_b, lru_wa, lru_ba, lru_wx, lru_bx, lru_lam, qnorm_g, knorm_g, sink_c, w_oa, w_ob, w_oc, w_out, w_up, ffn_conv_w, ffn_conv_b, w_down, final_g):
    batch, seq, d = x_prompt.shape
    dec_batch, dec_seq, _ = x_sample.shape
    depth = w_in.shape[0]
    assert d == D_MODEL and depth == DEPTH and dec_batch + 1 <= 8

    cond8 = jnp.zeros((8, d), F32).at[0].set(c_ctx).at[1:1 + dec_batch].set(c)
    mod_all = _mod_call(cond8, w_mod, b_mod).reshape(depth, 8, 6, d)

    w_in_b, w_oa_b, w_ob_b, w_oc_b = (w.astype(BF16) for w in (w_in, w_oa, w_ob, w_oc))
    w_out_b, w_up_b, w_down_b = (w.astype(BF16) for w in (w_out, w_up, w_down))
    w_gate = (0.5 * jnp.concatenate([lru_wa, lru_wx], axis=-1)).astype(BF16)
    half_ba, half_bx = 0.5 * lru_ba, 0.5 * lru_bx
    seg_q, seg_k = _segment_mean_matrix(D_Q), _segment_mean_matrix(D_KV)
    rope_tabs = _rope_tables(dec_seq)
    qg = jnp.tile(qnorm_g, (1, N_HEADS))
    kg = jnp.tile(knorm_g, (1, N_KV))

    def layer(l, x, mod, seq_len, h0, ctx, is_last):
        is_ctx = ctx is None
        ntok = x.shape[0]
        outs = _in_proj_call(l, x, mod, norm1_g[l][None], w_in_b, qg[l][None], kg[l][None], seg_q, seg_k,
                             None if is_ctx else rope_tabs, seq_len, emit_kv=is_ctx)
        xa, gya, qb, kbd, vbd, qc, kcd, vcd, sga, sgb, sgc = outs[:11]
        per_seq = lambda a: a.reshape(ntok // seq_len, seq_len, D_LRU)
        lru_args = lambda dr: (lru_conv_w[l], lru_conv_b[l][None], w_gate, half_ba[l, dr][None],
                               half_bx[l, dr][None], lru_lam[l, dr][None], h0[:, dr][:, None, :], seq_len)
        hf, last_f = _lru_call(l, per_seq(xa), *lru_args(0), reverse=False)
        oa, last_b = _lru_call(l, per_seq(xa), *lru_args(1), reverse=True, hf=hf, gya=per_seq(gya))
        oa = oa.reshape(ntok, D_LRU)
        if is_ctx:
            ob = _attn_call(qb, kbd, vbd, seq_len, "full")
            oc = _attn_call(qc, kcd, vcd, seq_len, "full", sink=sink_c[l])
        else:
            ob = _attn_call(qb, kbd, vbd, seq_len, "full", ctx=(ctx["kb"], ctx["vb"]))
            oc = _attn_call(qc, kcd, vcd, seq_len, "window", ctx=(ctx["kc"], ctx["vc"]), sink=sink_c[l])
        x = _mix_out_call(l, oa, ob, oc, sga, sgb, sgc, x, mod, w_oa_b, w_ob_b, w_oc_b, w_out_b, seq_len)
        x = _ffn_call(l, x, mod, norm2_g[l][None], w_up_b, ffn_conv_w[l], ffn_conv_b[l][None], w_down_b,
                      seq_len, final_g=final_g[None] if is_last else None)
        return x, outs[11:], (last_f, last_b)

    xp = x_prompt.reshape(batch * seq, d)
    xs = x_sample.reshape(dec_batch * dec_seq, d)
    zeros_h0 = jnp.zeros((batch, 2, D_LRU), F32)
    kbs, vbs, kcs, vcs, lrus = [], [], [], [], []
    for l in range(depth):
        is_last = l == depth - 1
        xp, (kb, vb, kc, vc), (last_f, last_b) = layer(l, xp, mod_all[l, 0:1], seq, zeros_h0, None, is_last)
        kbs.append(kb)
        vbs.append(vb)
        kcs.append(kc)
        vcs.append(vc)
        lrus.append(jnp.concatenate([last_f, last_b], axis=1))
        cached = {"kb": _dup_cache(cache_kb[:, l]), "vb": _ext_cache(cache_vb[:, l]),
                  "kc": _dup_cache(cache_kc[:, l]), "vc": _ext_cache(cache_vc[:, l])}
        xs, _, _ = layer(l, xs, mod_all[l, 1:1 + dec_batch], dec_seq, state_lru[:, l], cached, is_last)

    def stack_kv(parts):
        return jnp.stack([p.reshape(batch, seq, N_KV, HEAD_DIM) for p in parts], axis=1)

    y_prompt = xp.reshape(batch, seq, d)
    y_sample = xs.reshape(dec_batch, dec_seq, d)
    return (y_prompt, y_sample, stack_kv(kbs), stack_kv(vbs), stack_kv(kcs), stack_kv(vcs),
            jnp.stack(lrus, axis=1))
```

```python
import functools

import jax
import jax.numpy as jnp
import numpy as np
from jax import lax
from jax.experimental import pallas as pl
from jax.experimental.pallas import tpu as pltpu

D_MODEL = 1024
DEPTH = 2
GRID_W = 64
D_LRU = 1024
LRU_BLOCKS = 8
LRU_BLOCK = D_LRU // LRU_BLOCKS
LRU_C = 8.0
HEAD_DIM = 64
N_HEADS = 8
N_KV = 2
D_Q = N_HEADS * HEAD_DIM
D_KV = N_KV * HEAD_DIM
WINDOW = 128
D_FF = 2816
ROPE_THETA = 10000.0
NORM_EPS = 1e-6
NEG_INF = -1e30
ATTN_SCALE = HEAD_DIM ** -0.5
LOG2_E = 1.4426950408889634
LN_2 = 0.6931471805599453
IN_SIZES = (D_LRU, D_LRU, D_Q, D_KV, D_KV, D_Q, D_KV, D_KV, D_MODEL, D_MODEL, D_MODEL)
IN_OFF = tuple(int(v) for v in np.cumsum((0,) + IN_SIZES))
D_IN = IN_OFF[-1]

V7X_LANES = 128
V7X_SUBLANES_F32 = 8
V7X_VMEM_LIMIT_BYTES = 56 * 1024 * 1024

TOKEN_TILE = 512
FFN_CHUNK = 256
ATTN_Q_TILE = 256
ATTN_KEY_TILE = 256
ATTN_SEQS_PER_STEP = 4
ATTN_TILES_PER_STEP = 4
ATTN_LOOKAHEAD = 5
ATTN_ONES = 16
LRU_CHUNK = 512
LRU_SEQS_PER_STEP = 4
MOD_COL_TILE = 1536

F32 = jnp.float32
BF16 = jnp.bfloat16


def _params(*sem):
    return pltpu.CompilerParams(dimension_semantics=sem, vmem_limit_bytes=V7X_VMEM_LIMIT_BYTES)


def _resident(shape, lead=()):
    nd = len(shape)
    lead = tuple(int(i) for i in lead)
    return pl.BlockSpec((None,) * len(lead) + tuple(shape), lambda *_: lead + (0,) * nd,
                        pipeline_mode=pl.Buffered(1))


def _sigmoid(x):
    return 1.0 / (1.0 + jnp.exp(-x))


GELU_C0 = 0.7978845608028654
GELU_C1 = 0.044715 * GELU_C0


def _gelu_tanh(x):
    half = 0.5 * x
    return half + half * jnp.tanh(x * (GELU_C0 + GELU_C1 * (x * x)))


def _rms_rows(x):
    return x * lax.rsqrt(jnp.mean(x * x, axis=-1, keepdims=True) + NORM_EPS)


def _dot(a, b):
    return jnp.dot(a, b, preferred_element_type=F32)


def _mod_kernel(c_ref, w_ref, b_ref, o_ref):
    c = c_ref[...]
    s = (c * _sigmoid(c)).astype(BF16)
    o_ref[0] = _dot(s, w_ref[0].astype(BF16)) + b_ref[0]


def _mod_call(cond8, w_mod, b_mod):
    depth, d, n = w_mod.shape
    tn = MOD_COL_TILE
    return pl.pallas_call(
        _mod_kernel,
        grid=(depth, n // tn),
        in_specs=[
            pl.BlockSpec((8, d), lambda l, j: (0, 0)),
            pl.BlockSpec((1, d, tn), lambda l, j: (l, 0, j)),
            pl.BlockSpec((1, 1, tn), lambda l, j: (l, 0, j)),
        ],
        out_specs=pl.BlockSpec((1, 8, tn), lambda l, j: (l, 0, j)),
        out_shape=jax.ShapeDtypeStruct((depth, 8, n), F32),
        compiler_params=_params("parallel", "parallel"),
        name="mod",
    )(cond8, w_mod, b_mod.reshape(depth, 1, n))


def _head_rms(x, seg_ref, g):
    sq = x * x
    hi = sq.astype(BF16)
    lo = (sq - hi.astype(F32)).astype(BF16)
    ms = _dot(hi, seg_ref[...]) + _dot(lo, seg_ref[...])
    return x * lax.rsqrt(ms + NORM_EPS) * g


def _rope(x, cos, sin):
    w = x.shape[1]
    lane = lax.broadcasted_iota(jnp.int32, x.shape, 1)
    fwd = pltpu.roll(x, w - 16, 1)
    bwd = pltpu.roll(x, 16, 1)
    partner = jnp.where((lane % 32) < 16, fwd, bwd)
    return x * cos + partner * sin


def _dup_heads(x):
    lane = lax.broadcasted_iota(jnp.int32, x.shape, 1)
    swapped = pltpu.roll(x, HEAD_DIM, 1)
    low = lane < HEAD_DIM
    return jnp.concatenate([jnp.where(low, x, swapped), jnp.where(low, swapped, x)], axis=1)


def _ext_values(x):
    lane = lax.broadcasted_iota(jnp.int32, x.shape, 1)
    swapped = pltpu.roll(x, HEAD_DIM, 1)
    low = lane < HEAD_DIM
    tail = jnp.where(lane < HEAD_DIM + ATTN_ONES, 1.0, 0.0)
    return jnp.concatenate([jnp.where(low, x, tail), jnp.where(low, swapped, tail)], axis=1)


def _lru_conv(ext, cw, cb, n):
    halo = V7X_SUBLANES_F32
    taps = (pltpu.roll(ext, 2, 0), pltpu.roll(ext, 1, 0), ext, pltpu.roll(ext, n + 2 * halo - 1, 0))
    out = cb + cw[0:1] * taps[0][halo:halo + n]
    for k in range(1, 4):
        out = out + cw[k:k + 1] * taps[k][halo:halo + n]
    return out


def _in_proj_kernel(*refs, tm, seq_len, rope, emit_kv):
    span_tiles = seq_len > tm
    it = iter(refs)
    x_ref = next(it)
    prev_ref, next_ref = (next(it), next(it)) if span_tiles else (None, None)
    mod_ref, g_ref, w_ref, qg_ref, kg_ref, seg_q_ref, seg_k_ref, cw_ref, cb_ref = (next(it) for _ in range(9))
    cos_ref = sin_ref = None
    if rope:
        cos_ref, sin_ref = next(it), next(it)
    (xc_ref, gya_ref, qb_ref, kbd_ref, vbd_ref, qc_ref, kcd_ref, vcd_ref,
     sga_ref, sgb_ref, sgc_ref) = (next(it) for _ in range(11))
    if emit_kv:
        kb_ref, vb_ref, kc_ref, vc_ref = (next(it) for _ in range(4))

    mod = mod_ref[0]
    shift, scale = mod[0:1], mod[1:2]
    gain = g_ref[...] * (1.0 + scale)

    def modulated(v):
        return (_rms_rows(v) * gain + shift).astype(BF16)

    h = modulated(x_ref[...])

    def seg(lo, hi):
        return _dot(h, w_ref[:, IN_OFF[lo]:IN_OFF[hi]])

    halo = V7X_SUBLANES_F32
    xa = seg(0, 1)
    cw, cb = cw_ref[...], cb_ref[...]
    if span_tiles:
        tile_in_seq = pl.program_id(0) % (seq_len // tm)
        h_halo = modulated(jnp.concatenate([prev_ref[...], next_ref[...]], axis=0))
        xa_halo = _dot(h_halo, w_ref[:, IN_OFF[0]:IN_OFF[1]])
        ext = jnp.concatenate([jnp.where(tile_in_seq == 0, 0.0, xa_halo[:halo]), xa,
                               jnp.where(tile_in_seq == seq_len // tm - 1, 0.0, xa_halo[halo:])], axis=0)
        xc_ref[...] = _lru_conv(ext, cw, cb, tm)
    else:
        zero_halo = jnp.zeros((halo, D_LRU), F32)
        for b in range(tm // seq_len):
            rows = slice(b * seq_len, (b + 1) * seq_len)
            ext = jnp.concatenate([zero_halo, xa[rows], zero_halo], axis=0)
            xc_ref[rows, :] = _lru_conv(ext, cw, cb, seq_len)
    gya_ref[...] = _gelu_tanh(seg(1, 2)).astype(BF16)
    sga_ref[...] = _sigmoid(seg(8, 9)).astype(BF16)
    sgb_ref[...] = _sigmoid(seg(9, 10)).astype(BF16)
    sgc_ref[...] = _sigmoid(seg(10, 11)).astype(BF16)

    qb = _head_rms(seg(2, 3), seg_q_ref, qg_ref[...])
    kv_b = seg(3, 5)
    kb = _head_rms(kv_b[:, :D_KV], seg_k_ref, kg_ref[...])
    vb = kv_b[:, D_KV:]
    qkv_c = seg(5, 8)
    qc = qkv_c[:, :D_Q]
    kc = qkv_c[:, D_Q:D_Q + D_KV]
    vc = qkv_c[:, D_Q + D_KV:]
    if emit_kv:
        kb_ref[...] = kb
        vb_ref[...] = vb
        kc_ref[...] = kc
        vc_ref[...] = vc
    if rope:
        cos, sin = cos_ref[...], sin_ref[...]
        cos_q = jnp.concatenate([cos] * (D_Q // V7X_LANES), axis=1)
        sin_q = jnp.concatenate([sin] * (D_Q // V7X_LANES), axis=1)
        qb = _rope(qb, cos_q, sin_q)
        qc = _rope(qc, cos_q, sin_q)
        kb = _rope(kb, cos, sin)
        kc = _rope(kc, cos, sin)
    qb_ref[...] = (qb * (ATTN_SCALE * LOG2_E)).astype(BF16)
    qc_ref[...] = (qc * (ATTN_SCALE * LOG2_E)).astype(BF16)
    kbd_ref[...] = _dup_heads(kb).astype(BF16)
    vbd_ref[...] = _ext_values(vb).astype(BF16)
    kcd_ref[...] = _dup_heads(kc).astype(BF16)
    vcd_ref[...] = _ext_values(vc).astype(BF16)


def _in_proj_call(layer, x, mod, g, w_in, qg, kg, seg_q, seg_k, conv_w, conv_b, rope_tabs, seq_len, emit_kv):
    ntok = x.shape[0]
    tm = TOKEN_TILE
    halo = V7X_SUBLANES_F32
    tiles_per_seq = max(seq_len // tm, 1)
    ncond = mod.shape[0]
    cond_of = (lambda i: i // tiles_per_seq) if ncond > 1 else (lambda i: 0)
    row = lambda w: pl.BlockSpec((tm, w), lambda i: (i, 0))
    in_specs = [row(D_MODEL)]
    args = [x]
    if seq_len > tm:
        n_halo_blocks = ntok // halo
        in_specs += [
            pl.BlockSpec((halo, D_MODEL), lambda i: (jnp.maximum(i * (tm // halo) - 1, 0), 0)),
            pl.BlockSpec((halo, D_MODEL), lambda i: (jnp.minimum((i + 1) * (tm // halo), n_halo_blocks - 1), 0)),
        ]
        args += [x, x]
    in_specs += [
        pl.BlockSpec((1, 6, D_MODEL), lambda i: (cond_of(i), 0, 0)),
        _resident((1, D_MODEL)),
        _resident((D_MODEL, D_IN), lead=(layer,)),
        _resident((1, D_Q)),
        _resident((1, D_KV)),
        _resident((D_Q, D_Q)),
        _resident((D_KV, D_KV)),
        _resident((4, D_LRU)),
        _resident((1, D_LRU)),
    ]
    args += [mod, g, w_in, qg, kg, seg_q, seg_k, conv_w, conv_b]
    rope = rope_tabs is not None
    if rope:
        in_specs += [pl.BlockSpec((tm, V7X_LANES), lambda i: (i % tiles_per_seq, 0))] * 2
        args += list(rope_tabs)
    widths = [D_LRU, D_LRU, D_Q, 2 * D_KV, 2 * D_KV, D_Q, 2 * D_KV, 2 * D_KV, D_MODEL, D_MODEL, D_MODEL]
    out_specs = [row(w) for w in widths]
    out_shape = [jax.ShapeDtypeStruct((ntok, w), F32 if n == 0 else BF16) for n, w in enumerate(widths)]
    if emit_kv:
        out_specs += [row(D_KV)] * 4
        out_shape += [jax.ShapeDtypeStruct((ntok, D_KV), F32)] * 4
    return pl.pallas_call(
        functools.partial(_in_proj_kernel, tm=tm, seq_len=seq_len, rope=rope, emit_kv=emit_kv),
        grid=(ntok // tm,),
        in_specs=in_specs,
        out_specs=out_specs,
        out_shape=out_shape,
        compiler_params=_params("parallel"),
        name="in_proj",
    )(*args)


def _lru_kernel(*refs, chunk, n_chunks, n_seq, reverse):
    it = iter(refs)
    xc_ref, wg_ref, ba_ref, bx_ref, lam_ref, h0_ref = (next(it) for _ in range(6))
    if reverse:
        hf_ref, gya_ref = next(it), next(it)
    out_ref, last_ref = next(it), next(it)
    a_ref, u_ref, state_ref = next(it), next(it), next(it)
    hs_ref = next(it) if reverse else out_ref

    j = pl.program_id(1)
    lam = lam_ref[...]
    softplus_neg_lam = jnp.maximum(-lam, 0.0) + jnp.log1p(jnp.exp(-jnp.abs(lam)))
    slope = (-0.5 * LRU_C * LOG2_E) * softplus_neg_lam

    for s in range(n_seq):
        xc = xc_ref[s]
        xc_b = xc.astype(BF16)
        half_xc = 0.5 * xc

        for n in range(LRU_BLOCKS):
            cols = slice(n * LRU_BLOCK, (n + 1) * LRU_BLOCK)
            z = _dot(xc_b[:, cols], wg_ref[n])
            t_r = jnp.tanh(z[:, :LRU_BLOCK] + ba_ref[:, cols])
            t_i = jnp.tanh(z[:, LRU_BLOCK:] + bx_ref[:, cols])
            log2_a = slope[:, cols] + slope[:, cols] * t_r
            a = jnp.exp2(log2_a)
            one_minus_a2 = jnp.tanh(log2_a * (-LN_2)) * (a * a + 1.0)
            root = jnp.where(one_minus_a2 > 0.0, one_minus_a2 * lax.rsqrt(one_minus_a2), 0.0)
            hx = half_xc[:, cols]
            a_ref[s, :, cols] = a
            u_ref[s, :, cols] = root * (hx + hx * t_i)

    @pl.when(j == 0)
    def _():
        state_ref[...] = h0_ref[...]

    def step(t, hs):
        idx = (chunk - 1 - t) if reverse else t
        new = []
        for s in range(n_seq):
            h = a_ref[s, pl.ds(idx, 1), :] * hs[s] + u_ref[s, pl.ds(idx, 1), :]
            hs_ref[s, pl.ds(idx, 1), :] = h
            new.append(h)
        return tuple(new)

    hs = lax.fori_loop(0, chunk, step, tuple(state_ref[s] for s in range(n_seq)), unroll=8)
    for s in range(n_seq):
        state_ref[s] = hs[s]

    if reverse:
        out_ref[...] = ((hf_ref[...] + hs_ref[...]) * gya_ref[...].astype(F32)).astype(out_ref.dtype)

    @pl.when(j == n_chunks - 1)
    def _():
        for s in range(n_seq):
            last_ref[s] = hs[s]


def _lru_call(layer, xc, w_gate, ba, bx, lam, h0, seq_len, reverse, hf=None, gya=None):
    nb = xc.shape[0]
    chunk = min(LRU_CHUNK, seq_len)
    nt = seq_len // chunk
    n_seq = min(LRU_SEQS_PER_STEP, nb)
    pos = (lambda j: nt - 1 - j) if reverse else (lambda j: j)
    cur = lambda b, j: (b, pos(j), 0)
    per_seq = pl.BlockSpec((n_seq, 1, D_LRU), lambda b, j: (b, 0, 0))
    in_specs = [
        pl.BlockSpec((n_seq, chunk, D_LRU), cur),
        _resident((LRU_BLOCKS, LRU_BLOCK, 2 * LRU_BLOCK), lead=(layer, int(reverse))),
        _resident((1, D_LRU)),
        _resident((1, D_LRU)),
        _resident((1, D_LRU)),
        per_seq,
    ]
    args = [xc, w_gate, ba, bx, lam, h0]
    scratch = [
        pltpu.VMEM((n_seq, chunk, D_LRU), F32),
        pltpu.VMEM((n_seq, chunk, D_LRU), F32),
        pltpu.VMEM((n_seq, 1, D_LRU), F32),
    ]
    if reverse:
        in_specs += [pl.BlockSpec((n_seq, chunk, D_LRU), cur)] * 2
        args += [hf, gya]
        scratch.append(pltpu.VMEM((n_seq, chunk, D_LRU), F32))
    return pl.pallas_call(
        functools.partial(_lru_kernel, chunk=chunk, n_chunks=nt, n_seq=n_seq, reverse=reverse),
        grid=(nb // n_seq, nt),
        in_specs=in_specs,
        out_specs=[pl.BlockSpec((n_seq, chunk, D_LRU), cur), per_seq],
        out_shape=[jax.ShapeDtypeStruct((nb, seq_len, D_LRU), BF16 if reverse else F32),
                   jax.ShapeDtypeStruct((nb, 1, D_LRU), F32)],
        scratch_shapes=scratch,
        compiler_params=_params("arbitrary", "arbitrary"),
        name="lru_bwd" if reverse else "lru_fwd",
    )(*args)


def _attn_kernel(*refs, tq, seq_len, n_seq, mode, has_ctx, has_sink):
    it = iter(refs)
    q_ref, k_ref, v_ref = next(it), next(it), next(it)
    ck_ref = cv_ref = sink_ref = None
    if has_ctx:
        ck_ref, cv_ref = next(it), next(it)
    if has_sink:
        sink_ref = next(it)
    o_ref = next(it)
    m_ref, acc_ref = next(it), next(it)

    qi = pl.program_id(1)
    group = N_HEADS // N_KV
    acc_rows = HEAD_DIM + ATTN_ONES
    acc_row = lax.broadcasted_iota(jnp.int32, (acc_rows, tq), 0)
    lane = lax.broadcasted_iota(jnp.int32, (tq, V7X_LANES), 1)
    low = lane < HEAD_DIM
    key_tile = ATTN_KEY_TILE

    slots = [(s, h) for s in range(n_seq) for h in range(N_HEADS)]
    qh = []
    for slot, (s, h) in enumerate(slots):
        qp = q_ref[s * tq:(s + 1) * tq, (h // 2) * V7X_LANES:(h // 2 + 1) * V7X_LANES]
        zero = jnp.zeros_like(qp)
        qh.append(jnp.where(low, zero, qp) if h % 2 else jnp.where(low, qp, zero))
        if has_sink:
            m_ref[slot] = jnp.full((1, tq), sink_ref[h] * LOG2_E, F32)
            acc_ref[slot] = jnp.where(acc_row < HEAD_DIM, 0.0, 1.0)
        else:
            m_ref[slot] = jnp.full((1, tq), NEG_INF, F32)
            acc_ref[slot] = jnp.zeros((acc_rows, tq), F32)

    def scores(item):
        slot, key_ref, _, keys, _ = item
        g = slots[slot][1] // group
        k = key_ref[keys, g * V7X_LANES:(g + 1) * V7X_LANES]
        return lax.dot_general(k, qh[slot], (((1,), (1,)), ((), ())), preferred_element_type=F32)

    def absorb(item, st):
        slot, _, val_ref, keys, mask = item
        g = slots[slot][1] // group
        vt = val_ref[keys, g * V7X_LANES:(g + 1) * V7X_LANES].T[:acc_rows]
        if mask is not None:
            st = jnp.where(mask, st, NEG_INF)
        m_old = m_ref[slot]
        m_new = jnp.maximum(m_old, jnp.max(st, axis=0, keepdims=True))
        alpha = jnp.exp2(m_old - m_new)
        p = jnp.exp2(st - m_new)
        acc_ref[slot] = alpha * acc_ref[slot] + _dot(vt, p.astype(BF16))
        m_ref[slot] = m_new

    def run(items):
        pending = {}
        for i in range(len(items) + ATTN_LOOKAHEAD):
            if i < len(items):
                pending[i] = scores(items[i])
            j = i - ATTN_LOOKAHEAD
            if j >= 0:
                absorb(items[j], pending.pop(j))

    def tile_items(key_ref, val_ref, keys_of_seq, mask):
        return [(slot, key_ref, val_ref, keys_of_seq(s), mask) for slot, (s, _) in enumerate(slots)]

    head_items = tile_items(ck_ref, cv_ref, lambda s: slice(None), None) if has_ctx else []
    if mode == "full":
        n_tiles = seq_len // key_tile
        if n_tiles == 1:
            run(head_items + tile_items(k_ref, v_ref, lambda s: slice(s * seq_len, (s + 1) * seq_len), None))
        else:
            run(head_items)
            per_step = ATTN_TILES_PER_STEP

            def body(t, carry):
                items = []
                for u in range(per_step):
                    keys = pl.ds(pl.multiple_of((t * per_step + u) * key_tile, key_tile), key_tile)
                    items += tile_items(k_ref, v_ref, lambda s: keys, None)
                run(items)
                return carry
            lax.fori_loop(0, n_tiles // per_step, body, 0)
    else:
        span = tq + 2 * WINDOW
        q0 = qi * tq
        start = pl.multiple_of(jnp.clip(q0 - WINDOW, 0, seq_len - span), WINDOW)
        items = head_items
        for u in range(span // key_tile):
            k_abs = start + u * key_tile + lax.broadcasted_iota(jnp.int32, (key_tile, tq), 0)
            q_abs = q0 + lax.broadcasted_iota(jnp.int32, (key_tile, tq), 1)
            keys = pl.ds(start + u * key_tile, key_tile)
            items = items + tile_items(k_ref, v_ref, lambda s: keys, jnp.abs(q_abs - k_abs) <= WINDOW)
        run(items)

    def normalised(slot):
        acc = acc_ref[slot]
        return (acc[:HEAD_DIM] * (1.0 / acc[HEAD_DIM:HEAD_DIM + 1])).T

    for s in range(n_seq):
        for pair in range(N_HEADS // 2):
            halves = [normalised(s * N_HEADS + 2 * pair), normalised(s * N_HEADS + 2 * pair + 1)]
            o_ref[s * tq:(s + 1) * tq, pair * V7X_LANES:(pair + 1) * V7X_LANES] = (
                jnp.concatenate(halves, axis=1).astype(o_ref.dtype))


def _attn_call(q, kd, vd, seq_len, mode, ctx=None, sink=None):
    ntok = q.shape[0]
    nb = ntok // seq_len
    tq = min(ATTN_Q_TILE, seq_len)
    nq = seq_len // tq
    n_seq = ATTN_SEQS_PER_STEP if nq == 1 else 1
    group = N_HEADS // N_KV
    in_specs = [
        pl.BlockSpec((n_seq * tq, D_Q), lambda b, i: (b * nq + i, 0)),
        pl.BlockSpec((n_seq * seq_len, 2 * D_KV), lambda b, i: (b, 0)),
        pl.BlockSpec((n_seq * seq_len, 2 * D_KV), lambda b, i: (b, 0)),
    ]
    args = [q, kd, vd]
    if ctx is not None:
        assert n_seq == 1
        past = ctx[0].shape[1]
        in_specs += [pl.BlockSpec((None, past, 2 * D_KV), lambda b, i: (b, 0, 0))] * 2
        args += list(ctx)
    if sink is not None:
        in_specs.append(pl.BlockSpec(memory_space=pltpu.SMEM))
        args.append(sink)
    return pl.pallas_call(
        functools.partial(_attn_kernel, tq=tq, seq_len=seq_len, n_seq=n_seq, mode=mode,
                          has_ctx=ctx is not None, has_sink=sink is not None),
        grid=(nb // n_seq, nq),
        in_specs=in_specs,
        out_specs=pl.BlockSpec((n_seq * tq, D_Q), lambda b, i: (b * nq + i, 0)),
        out_shape=jax.ShapeDtypeStruct((ntok, D_Q), BF16),
        scratch_shapes=[
            pltpu.VMEM((n_seq * N_HEADS, 1, tq), F32),
            pltpu.VMEM((n_seq * N_HEADS, HEAD_DIM + ATTN_ONES, tq), F32),
        ],
        compiler_params=_params("parallel", "parallel"),
        name="attn_" + mode + ("_sink" if sink is not None else ""),
    )(*args)


def _mix_out_kernel(oa_ref, ob_ref, oc_ref, sga_ref, sgb_ref, sgc_ref, x_ref, mod_ref,
                    woa_ref, wob_ref, woc_ref, wout_ref, o_ref):
    merged = (sga_ref[...].astype(F32) * _dot(oa_ref[...], woa_ref[...])
              + sgb_ref[...].astype(F32) * _dot(ob_ref[...], wob_ref[...])
              + sgc_ref[...].astype(F32) * _dot(oc_ref[...], woc_ref[...]))
    out = _dot(merged.astype(BF16), wout_ref[...])
    o_ref[...] = x_ref[...] + mod_ref[0][2:3] * out


def _mix_out_call(layer, oa, ob, oc, sga, sgb, sgc, x, mod, w_oa, w_ob, w_oc, w_out, seq_len):
    ntok = x.shape[0]
    tm = TOKEN_TILE
    tiles_per_seq = max(seq_len // tm, 1)
    cond_of = (lambda i: i // tiles_per_seq) if mod.shape[0] > 1 else (lambda i: 0)
    row = lambda w: pl.BlockSpec((tm, w), lambda i: (i, 0))
    return pl.pallas_call(
        _mix_out_kernel,
        grid=(ntok // tm,),
        in_specs=[row(D_LRU), row(D_Q), row(D_Q), row(D_MODEL), row(D_MODEL), row(D_MODEL), row(D_MODEL),
                  pl.BlockSpec((1, 6, D_MODEL), lambda i: (cond_of(i), 0, 0)),
                  _resident((D_LRU, D_MODEL), lead=(layer,)), _resident((D_Q, D_MODEL), lead=(layer,)),
                  _resident((D_Q, D_MODEL), lead=(layer,)), _resident((D_MODEL, D_MODEL), lead=(layer,))],
        out_specs=row(D_MODEL),
        out_shape=jax.ShapeDtypeStruct((ntok, D_MODEL), F32),
        compiler_params=_params("parallel"),
        name="mix_out",
    )(oa, ob, oc, sga, sgb, sgc, x, mod, w_oa, w_ob, w_oc, w_out)


def _ffn_kernel(*refs, tm, seq_len, final):
    span_tiles = seq_len > tm
    it = iter(refs)
    x_ref = next(it)
    prev_ref, next_ref = (next(it), next(it)) if span_tiles else (None, None)
    mod_ref, g_ref, wup_ref, cw_ref, cb_ref, wdn_ref = (next(it) for _ in range(6))
    fg_ref = next(it) if final else None
    o_ref = next(it)
    h_ref, act_ref = next(it), next(it)

    halo = V7X_SUBLANES_F32
    sub = min(tm, seq_len)
    mod = mod_ref[0]
    shift, scale, gate2 = mod[3:4], mod[4:5], mod[5:6]
    gain = g_ref[...] * (1.0 + scale)
    x = x_ref[...]

    def modulated(v):
        return (_rms_rows(v) * gain + shift).astype(BF16)

    if span_tiles:
        tile_in_seq = pl.program_id(0) % (seq_len // tm)
        at_start, at_end = tile_in_seq == 0, tile_in_seq == seq_len // tm - 1
        h_ref[0:halo, :] = modulated(prev_ref[...])
        h_ref[halo:halo + tm, :] = modulated(x)
        h_ref[halo + tm:, :] = modulated(next_ref[...])
        centre = slice(halo, halo + tm)
    else:
        h_ref[...] = modulated(x)
        centre = slice(None)

    cw = cw_ref[...]
    zero_halo = jnp.zeros((halo, FFN_CHUNK), F32)
    for c in range(D_FF // FFN_CHUNK):
        cols = slice(c * FFN_CHUNK, (c + 1) * FFN_CHUNK)
        vcols = slice(D_FF + c * FFN_CHUNK, D_FF + (c + 1) * FFN_CHUNK)
        pre = _dot(h_ref[...], wup_ref[:, cols])
        val = _dot(h_ref[centre, :], wup_ref[:, vcols])
        parts = []
        for b in range(tm // sub):
            if span_tiles:
                ext = jnp.concatenate([jnp.where(at_start, 0.0, pre[:halo]), pre[halo:halo + tm],
                                       jnp.where(at_end, 0.0, pre[halo + tm:])], axis=0)
            else:
                ext = jnp.concatenate([zero_halo, pre[b * sub:(b + 1) * sub], zero_halo], axis=0)
            before = pltpu.roll(ext, 1, 0)[halo:halo + sub]
            after = pltpu.roll(ext, sub + 2 * halo - 1, 0)[halo:halo + sub]
            parts.append(cb_ref[:, cols] + cw[0:1, cols] * before + cw[1:2, cols] * ext[halo:halo + sub]
                         + cw[2:3, cols] * after)
        gate = parts[0] if len(parts) == 1 else jnp.concatenate(parts, axis=0)
        act_ref[:, cols] = (_gelu_tanh(gate) * val).astype(BF16)

    y = x + gate2 * _dot(act_ref[...], wdn_ref[...])
    if final:
        y = _rms_rows(y) * fg_ref[...]
    o_ref[...] = y


def _ffn_call(layer, x, mod, g, w_up, conv_w, conv_b, w_down, seq_len, final_g=None):
    ntok = x.shape[0]
    tm = TOKEN_TILE
    halo = V7X_SUBLANES_F32
    tiles_per_seq = max(seq_len // tm, 1)
    cond_of = (lambda i: i // tiles_per_seq) if mod.shape[0] > 1 else (lambda i: 0)
    n_halo_blocks = ntok // halo
    span_tiles = seq_len > tm
    in_specs = [pl.BlockSpec((tm, D_MODEL), lambda i: (i, 0))]
    args = [x]
    if span_tiles:
        in_specs += [
            pl.BlockSpec((halo, D_MODEL), lambda i: (jnp.maximum(i * (tm // halo) - 1, 0), 0)),
            pl.BlockSpec((halo, D_MODEL), lambda i: (jnp.minimum((i + 1) * (tm // halo), n_halo_blocks - 1), 0)),
        ]
        args += [x, x]
    in_specs += [
        pl.BlockSpec((1, 6, D_MODEL), lambda i: (cond_of(i), 0, 0)),
        _resident((1, D_MODEL)),
        _resident((D_MODEL, 2 * D_FF), lead=(layer,)),
        _resident((3, D_FF)),
        _resident((1, D_FF)),
        _resident((D_FF, D_MODEL), lead=(layer,)),
    ]
    args += [mod, g, w_up, conv_w, conv_b, w_down]
    h_rows = tm + 2 * halo if span_tiles else tm
    final = final_g is not None
    if final:
        in_specs.append(_resident((1, D_MODEL)))
        args.append(final_g)
    return pl.pallas_call(
        functools.partial(_ffn_kernel, tm=tm, seq_len=seq_len, final=final),
        grid=(ntok // tm,),
        in_specs=in_specs,
        out_specs=pl.BlockSpec((tm, D_MODEL), lambda i: (i, 0)),
        out_shape=jax.ShapeDtypeStruct((ntok, D_MODEL), F32),
        scratch_shapes=[
            pltpu.VMEM((h_rows, D_MODEL), BF16),
            pltpu.VMEM((tm, D_FF), BF16),
        ],
        compiler_params=_params("parallel"),
        name="ffn",
    )(*args)


def _rope_tables(seq_len):
    nf = HEAD_DIM // 4
    n_rows = seq_len // GRID_W
    inv = ROPE_THETA ** (-jnp.arange(nf, dtype=F32) / nf)
    ang_r = jnp.arange(n_rows, dtype=F32)[:, None] * inv[None, :]
    ang_c = jnp.arange(GRID_W, dtype=F32)[:, None] * inv[None, :]
    by_row = lambda a: jnp.repeat(a, GRID_W, axis=0)
    by_col = lambda a: jnp.tile(a, (n_rows, 1))
    cos_r, sin_r, cos_c, sin_c = by_row(jnp.cos(ang_r)), by_row(jnp.sin(ang_r)), by_col(jnp.cos(ang_c)), by_col(jnp.sin(ang_c))
    cos = jnp.concatenate([cos_r, cos_r, cos_c, cos_c], axis=1)
    sin = jnp.concatenate([-sin_r, sin_r, -sin_c, sin_c], axis=1)
    reps = V7X_LANES // HEAD_DIM
    return jnp.concatenate([cos] * reps, axis=1), jnp.concatenate([sin] * reps, axis=1)


def _segment_mean_matrix(width):
    seg = np.arange(width) // HEAD_DIM
    return jnp.asarray((seg[:, None] == seg[None, :]).astype(np.float32) / HEAD_DIM, dtype=BF16)


def _dup_cache(c):
    return jnp.concatenate([c[:, :, 0], c[:, :, 0], c[:, :, 1], c[:, :, 1]], axis=-1).astype(BF16)


def _ext_cache(c):
    b, n = c.shape[:2]
    ones = jnp.ones((b, n, ATTN_ONES), c.dtype)
    zeros = jnp.zeros((b, n, V7X_LANES - HEAD_DIM - ATTN_ONES), c.dtype)
    return jnp.concatenate([c[:, :, 0], ones, zeros, c[:, :, 1], ones, zeros], axis=-1).astype(BF16)


def kernel(x_prompt, x_sample, c, cache_kb, cache_vb, cache_kc, cache_vc, state_lru, c_ctx, norm1_g, norm2_g, w_mod, b_mod, w_in, lru_conv_w, lru_conv_b, lru_wa, lru_ba, lru_wx, lru_bx, lru_lam, qnorm_g, knorm_g, sink_c, w_oa, w_ob, w_oc, w_out, w_up, ffn_conv_w, ffn_conv_b, w_down, final_g):
    batch, seq, d = x_prompt.shape
    dec_batch, dec_seq, _ = x_sample.shape
    depth = w_in.shape[0]
    assert d == D_MODEL and depth == DEPTH and dec_batch + 1 <= 8

    cond8 = jnp.zeros((8, d), F32).at[0].set(c_ctx).at[1:1 + dec_batch].set(c)
    mod_all = _mod_call(cond8, w_mod, b_mod).reshape(depth, 8, 6, d)

    w_in_b, w_oa_b, w_ob_b, w_oc_b = (w.astype(BF16) for w in (w_in, w_oa, w_ob, w_oc))
    w_out_b, w_up_b, w_down_b = (w.astype(BF16) for w in (w_out, w_up, w_down))
    w_gate = (0.5 * jnp.concatenate([lru_wa, lru_wx], axis=-1)).astype(BF16)
    half_ba, half_bx = 0.5 * lru_ba, 0.5 * lru_bx
    seg_q, seg_k = _segment_mean_matrix(D_Q), _segment_mean_matrix(D_KV)
    rope_tabs = _rope_tables(dec_seq)
    qg = jnp.tile(qnorm_g, (1, N_HEADS))
    kg = jnp.tile(knorm_g, (1, N_KV))

    def layer(l, x, mod, seq_len, h0, ctx, is_last):
        is_ctx = ctx is None
        ntok = x.shape[0]
        outs = _in_proj_call(l, x, mod, norm1_g[l][None], w_in_b, qg[l][None], kg[l][None], seg_q, seg_k,
                             lru_conv_w[l], lru_conv_b[l][None], None if is_ctx else rope_tabs, seq_len,
                             emit_kv=is_ctx)
        xc, gya, qb, kbd, vbd, qc, kcd, vcd, sga, sgb, sgc = outs[:11]
        per_seq = lambda a: a.reshape(ntok // seq_len, seq_len, D_LRU)
        lru_args = lambda dr: (w_gate, half_ba[l, dr][None], half_bx[l, dr][None], lru_lam[l, dr][None],
                               h0[:, dr][:, None, :], seq_len)
        hf, last_f = _lru_call(l, per_seq(xc), *lru_args(0), reverse=False)
        oa, last_b = _lru_call(l, per_seq(xc), *lru_args(1), reverse=True, hf=hf, gya=per_seq(gya))
        oa = oa.reshape(ntok, D_LRU)
        if is_ctx:
            ob = _attn_call(qb, kbd, vbd, seq_len, "full")
            oc = _attn_call(qc, kcd, vcd, seq_len, "full", sink=sink_c[l])
        else:
            ob = _attn_call(qb, kbd, vbd, seq_len, "full", ctx=(ctx["kb"], ctx["vb"]))
            oc = _attn_call(qc, kcd, vcd, seq_len, "window", ctx=(ctx["kc"], ctx["vc"]), sink=sink_c[l])
        x = _mix_out_call(l, oa, ob, oc, sga, sgb, sgc, x, mod, w_oa_b, w_ob_b, w_oc_b, w_out_b, seq_len)
        x = _ffn_call(l, x, mod, norm2_g[l][None], w_up_b, ffn_conv_w[l], ffn_conv_b[l][None], w_down_b,
                      seq_len, final_g=final_g[None] if is_last else None)
        return x, outs[11:], (last_f, last_b)

    xp = x_prompt.reshape(batch * seq, d)
    xs = x_sample.reshape(dec_batch * dec_seq, d)
    zeros_h0 = jnp.zeros((batch, 2, D_LRU), F32)
    kbs, vbs, kcs, vcs, lrus = [], [], [], [], []
    for l in range(depth):
        is_last = l == depth - 1
        xp, (kb, vb, kc, vc), (last_f, last_b) = layer(l, xp, mod_all[l, 0:1], seq, zeros_h0, None, is_last)
        kbs.append(kb)
        vbs.append(vb)
        kcs.append(kc)
        vcs.append(vc)
        lrus.append(jnp.concatenate([last_f, last_b], axis=1))
        cached = {"kb": _dup_cache(cache_kb[:, l]), "vb": _ext_cache(cache_vb[:, l]),
                  "kc": _dup_cache(cache_kc[:, l]), "vc": _ext_cache(cache_vc[:, l])}
        xs, _, _ = layer(l, xs, mod_all[l, 1:1 + dec_batch], dec_seq, state_lru[:, l], cached, is_last)

    def stack_kv(parts):
        return jnp.stack([p.reshape(batch, seq, N_KV, HEAD_DIM) for p in parts], axis=1)

    y_prompt = xp.reshape(batch, seq, d)
    y_sample = xs.reshape(dec_batch, dec_seq, d)
    return (y_prompt, y_sample, stack_kv(kbs), stack_kv(vbs), stack_kv(kcs), stack_kv(vcs),
            jnp.stack(lrus, axis=1))
```

```python
import functools

import jax
import jax.numpy as jnp
import numpy as np
from jax import lax
from jax.experimental import pallas as pl
from jax.experimental.pallas import tpu as pltpu

D_MODEL = 1024
DEPTH = 2
GRID_W = 64
D_LRU = 1024
LRU_BLOCKS = 8
LRU_BLOCK = D_LRU // LRU_BLOCKS
LRU_C = 8.0
HEAD_DIM = 64
N_HEADS = 8
N_KV = 2
D_Q = N_HEADS * HEAD_DIM
D_KV = N_KV * HEAD_DIM
WINDOW = 128
D_FF = 2816
ROPE_THETA = 10000.0
NORM_EPS = 1e-6
NEG_INF = -1e30
ATTN_SCALE = HEAD_DIM ** -0.5
LOG2_E = 1.4426950408889634
LN_2 = 0.6931471805599453
IN_SIZES = (D_LRU, D_LRU, D_Q, D_KV, D_KV, D_Q, D_KV, D_KV, D_MODEL, D_MODEL, D_MODEL)
IN_OFF = tuple(int(v) for v in np.cumsum((0,) + IN_SIZES))
D_IN = IN_OFF[-1]
D_TOKEN_MIX = IN_OFF[8]

V7X_LANES = 128
V7X_SUBLANES_F32 = 8
V7X_VMEM_LIMIT_BYTES = 56 * 1024 * 1024

TOKEN_TILE = 512
FFN_CHUNK = 256
ATTN_Q_TILE = 256
ATTN_KEY_TILE = 256
ATTN_SEQS_PER_STEP = 4
ATTN_TILES_PER_STEP = 4
ATTN_LOOKAHEAD = 5
ATTN_ONES = 16
LRU_CHUNK = 512
LRU_SEQS_PER_STEP = 4
LRU_SCAN_UNROLL = 8
MOD_COL_TILE = 1536

F32 = jnp.float32
BF16 = jnp.bfloat16


def _params(*sem):
    return pltpu.CompilerParams(dimension_semantics=sem, vmem_limit_bytes=V7X_VMEM_LIMIT_BYTES)


def _resident(shape, lead=()):
    nd = len(shape)
    lead = tuple(int(i) for i in lead)
    return pl.BlockSpec((None,) * len(lead) + tuple(shape), lambda *_: lead + (0,) * nd,
                        pipeline_mode=pl.Buffered(1))


def _sigmoid(x):
    return 1.0 / (1.0 + jnp.exp(-x))


GELU_C0 = 0.7978845608028654
GELU_C1 = 0.044715 * GELU_C0


def _gelu_tanh(x):
    half = 0.5 * x
    return half + half * jnp.tanh(x * (GELU_C0 + GELU_C1 * (x * x)))


def _rms_rows(x):
    return x * lax.rsqrt(jnp.mean(x * x, axis=-1, keepdims=True) + NORM_EPS)


def _dot(a, b):
    return jnp.dot(a, b, preferred_element_type=F32)


def _mod_kernel(c_ref, w_ref, b_ref, o_ref):
    c = c_ref[...]
    s = (c * _sigmoid(c)).astype(BF16)
    o_ref[0] = _dot(s, w_ref[0].astype(BF16)) + b_ref[0]


def _mod_call(cond8, w_mod, b_mod):
    depth, d, n = w_mod.shape
    tn = MOD_COL_TILE
    return pl.pallas_call(
        _mod_kernel,
        grid=(depth, n // tn),
        in_specs=[
            pl.BlockSpec((8, d), lambda l, j: (0, 0)),
            pl.BlockSpec((1, d, tn), lambda l, j: (l, 0, j)),
            pl.BlockSpec((1, 1, tn), lambda l, j: (l, 0, j)),
        ],
        out_specs=pl.BlockSpec((1, 8, tn), lambda l, j: (l, 0, j)),
        out_shape=jax.ShapeDtypeStruct((depth, 8, n), F32),
        compiler_params=_params("parallel", "parallel"),
        name="mod",
    )(cond8, w_mod, b_mod.reshape(depth, 1, n))


def _head_rms(x, seg_ref, g):
    sq = x * x
    hi = sq.astype(BF16)
    lo = (sq - hi.astype(F32)).astype(BF16)
    ms = _dot(hi, seg_ref[...]) + _dot(lo, seg_ref[...])
    return x * lax.rsqrt(ms + NORM_EPS) * g


def _rope(x, cos, sin):
    w = x.shape[1]
    lane = lax.broadcasted_iota(jnp.int32, x.shape, 1)
    fwd = pltpu.roll(x, w - 16, 1)
    bwd = pltpu.roll(x, 16, 1)
    partner = jnp.where((lane % 32) < 16, fwd, bwd)
    return x * cos + partner * sin


def _dup_heads(x):
    lane = lax.broadcasted_iota(jnp.int32, x.shape, 1)
    swapped = pltpu.roll(x, HEAD_DIM, 1)
    low = lane < HEAD_DIM
    return jnp.concatenate([jnp.where(low, x, swapped), jnp.where(low, swapped, x)], axis=1)


def _ext_values(x):
    lane = lax.broadcasted_iota(jnp.int32, x.shape, 1)
    swapped = pltpu.roll(x, HEAD_DIM, 1)
    low = lane < HEAD_DIM
    tail = jnp.where(lane < HEAD_DIM + ATTN_ONES, 1.0, 0.0)
    return jnp.concatenate([jnp.where(low, x, tail), jnp.where(low, swapped, tail)], axis=1)


def _lru_conv(ext, cw, cb, n):
    halo = V7X_SUBLANES_F32
    taps = (pltpu.roll(ext, 2, 0), pltpu.roll(ext, 1, 0), ext, pltpu.roll(ext, n + 2 * halo - 1, 0))
    out = cb + cw[0:1] * taps[0][halo:halo + n]
    for k in range(1, 4):
        out = out + cw[k:k + 1] * taps[k][halo:halo + n]
    return out


def _in_proj_kernel(*refs, tm, seq_len, rope, emit_kv):
    span_tiles = seq_len > tm
    it = iter(refs)
    x_ref = next(it)
    prev_ref, next_ref = (next(it), next(it)) if span_tiles else (None, None)
    mod_ref, g_ref, w_ref, qg_ref, kg_ref, seg_q_ref, seg_k_ref, cw_ref, cb_ref = (next(it) for _ in range(9))
    cos_ref = sin_ref = None
    if rope:
        cos_ref, sin_ref = next(it), next(it)
    xc_ref, gya_ref, qb_ref, kbd_ref, vbd_ref, qc_ref, kcd_ref, vcd_ref = (next(it) for _ in range(8))
    if emit_kv:
        kb_ref, vb_ref, kc_ref, vc_ref = (next(it) for _ in range(4))

    mod = mod_ref[0]
    shift, scale = mod[0:1], mod[1:2]
    gain = g_ref[...] * (1.0 + scale)

    def modulated(v):
        return (_rms_rows(v) * gain + shift).astype(BF16)

    h = modulated(x_ref[...])

    def seg(lo, hi):
        return _dot(h, w_ref[:, IN_OFF[lo]:IN_OFF[hi]])

    halo = V7X_SUBLANES_F32
    qb_raw = seg(2, 3)
    kv_b = seg(3, 5)
    qkv_c = seg(5, 8)
    xa = seg(0, 1)
    if span_tiles:
        h_halo = modulated(jnp.concatenate([prev_ref[...], next_ref[...]], axis=0))
        xa_halo = _dot(h_halo, w_ref[:, IN_OFF[0]:IN_OFF[1]])
    qb = _head_rms(qb_raw, seg_q_ref, qg_ref[...])
    kb = _head_rms(kv_b[:, :D_KV], seg_k_ref, kg_ref[...])
    vb = kv_b[:, D_KV:]
    qc = qkv_c[:, :D_Q]
    kc = qkv_c[:, D_Q:D_Q + D_KV]
    vc = qkv_c[:, D_Q + D_KV:]
    gya_ref[...] = _gelu_tanh(seg(1, 2)).astype(BF16)

    cw, cb = cw_ref[...], cb_ref[...]
    if span_tiles:
        tile_in_seq = pl.program_id(0) % (seq_len // tm)
        ext = jnp.concatenate([jnp.where(tile_in_seq == 0, 0.0, xa_halo[:halo]), xa,
                               jnp.where(tile_in_seq == seq_len // tm - 1, 0.0, xa_halo[halo:])], axis=0)
        xc_ref[...] = _lru_conv(ext, cw, cb, tm)
    else:
        zero_halo = jnp.zeros((halo, D_LRU), F32)
        for b in range(tm // seq_len):
            rows = slice(b * seq_len, (b + 1) * seq_len)
            ext = jnp.concatenate([zero_halo, xa[rows], zero_halo], axis=0)
            xc_ref[rows, :] = _lru_conv(ext, cw, cb, seq_len)
    if emit_kv:
        kb_ref[...] = kb
        vb_ref[...] = vb
        kc_ref[...] = kc
        vc_ref[...] = vc
    if rope:
        cos, sin = cos_ref[...], sin_ref[...]
        cos_q = jnp.concatenate([cos] * (D_Q // V7X_LANES), axis=1)
        sin_q = jnp.concatenate([sin] * (D_Q // V7X_LANES), axis=1)
        qb = _rope(qb, cos_q, sin_q)
        qc = _rope(qc, cos_q, sin_q)
        kb = _rope(kb, cos, sin)
        kc = _rope(kc, cos, sin)
    qb_ref[...] = (qb * (ATTN_SCALE * LOG2_E)).astype(BF16)
    qc_ref[...] = (qc * (ATTN_SCALE * LOG2_E)).astype(BF16)
    kbd_ref[...] = _dup_heads(kb).astype(BF16)
    vbd_ref[...] = _ext_values(vb).astype(BF16)
    kcd_ref[...] = _dup_heads(kc).astype(BF16)
    vcd_ref[...] = _ext_values(vc).astype(BF16)


def _in_proj_call(layer, x, mod, g, w_in, qg, kg, seg_q, seg_k, conv_w, conv_b, rope_tabs, seq_len, emit_kv):
    ntok = x.shape[0]
    tm = TOKEN_TILE
    halo = V7X_SUBLANES_F32
    tiles_per_seq = max(seq_len // tm, 1)
    ncond = mod.shape[0]
    cond_of = (lambda i: i // tiles_per_seq) if ncond > 1 else (lambda i: 0)
    row = lambda w: pl.BlockSpec((tm, w), lambda i: (i, 0))
    in_specs = [row(D_MODEL)]
    args = [x]
    if seq_len > tm:
        n_halo_blocks = ntok // halo
        in_specs += [
            pl.BlockSpec((halo, D_MODEL), lambda i: (jnp.maximum(i * (tm // halo) - 1, 0), 0)),
            pl.BlockSpec((halo, D_MODEL), lambda i: (jnp.minimum((i + 1) * (tm // halo), n_halo_blocks - 1), 0)),
        ]
        args += [x, x]
    in_specs += [
        pl.BlockSpec((1, 6, D_MODEL), lambda i: (cond_of(i), 0, 0)),
        _resident((1, D_MODEL)),
        _resident((D_MODEL, D_TOKEN_MIX), lead=(layer,)),
        _resident((1, D_Q)),
        _resident((1, D_KV)),
        _resident((D_Q, D_Q)),
        _resident((D_KV, D_KV)),
        _resident((4, D_LRU)),
        _resident((1, D_LRU)),
    ]
    args += [mod, g, w_in, qg, kg, seg_q, seg_k, conv_w, conv_b]
    rope = rope_tabs is not None
    if rope:
        in_specs += [pl.BlockSpec((tm, V7X_LANES), lambda i: (i % tiles_per_seq, 0))] * 2
        args += list(rope_tabs)
    widths = [D_LRU, D_LRU, D_Q, 2 * D_KV, 2 * D_KV, D_Q, 2 * D_KV, 2 * D_KV]
    out_specs = [row(w) for w in widths]
    out_shape = [jax.ShapeDtypeStruct((ntok, w), F32 if n == 0 else BF16) for n, w in enumerate(widths)]
    if emit_kv:
        out_specs += [row(D_KV)] * 4
        out_shape += [jax.ShapeDtypeStruct((ntok, D_KV), F32)] * 4
    return pl.pallas_call(
        functools.partial(_in_proj_kernel, tm=tm, seq_len=seq_len, rope=rope, emit_kv=emit_kv),
        grid=(ntok // tm,),
        in_specs=in_specs,
        out_specs=out_specs,
        out_shape=out_shape,
        compiler_params=_params("parallel"),
        name="in_proj",
    )(*args)


def _lru_kernel(*refs, chunk, n_chunks, n_seq, reverse):
    it = iter(refs)
    xc_ref, wg_ref, ba_ref, bx_ref, lam_ref, h0_ref = (next(it) for _ in range(6))
    if reverse:
        hf_ref, gya_ref = next(it), next(it)
    out_ref, last_ref = next(it), next(it)
    a_ref, u_ref, hs_ref, state_ref = next(it), next(it), next(it), next(it)

    fold = D_LRU // V7X_LANES

    def folded_block(c):
        return pl.ds(c, chunk, stride=fold)

    j = pl.program_id(1)
    lam = lam_ref[...]
    softplus_neg_lam = jnp.maximum(-lam, 0.0) + jnp.log1p(jnp.exp(-jnp.abs(lam)))
    slope = (-0.5 * LRU_C * LOG2_E) * softplus_neg_lam

    for s in range(n_seq):
        xc = xc_ref[s]
        xc_b = xc.astype(BF16)
        half_xc = 0.5 * xc

        for n in range(LRU_BLOCKS):
            cols = slice(n * LRU_BLOCK, (n + 1) * LRU_BLOCK)
            z = _dot(xc_b[:, cols], wg_ref[n])
            t_r = jnp.tanh(z[:, :LRU_BLOCK] + ba_ref[:, cols])
            t_i = jnp.tanh(z[:, LRU_BLOCK:] + bx_ref[:, cols])
            log2_a = slope[:, cols] + slope[:, cols] * t_r
            a = jnp.exp2(log2_a)
            one_minus_a2 = jnp.tanh(log2_a * (-LN_2)) * (a * a + 1.0)
            root = jnp.where(one_minus_a2 > 0.0, one_minus_a2 * lax.rsqrt(one_minus_a2), 0.0)
            hx = half_xc[:, cols]
            a_ref[s, folded_block(n), :] = a
            u_ref[s, folded_block(n), :] = root * (hx + hx * t_i)

    @pl.when(j == 0)
    def _():
        state_ref[...] = h0_ref[...]

    steps_per_trip = LRU_SCAN_UNROLL

    def trip(gi, hs):
        gidx = (chunk // steps_per_trip - 1 - gi) if reverse else gi
        base = pl.multiple_of(gidx * (steps_per_trip * fold), steps_per_trip * fold)
        hs = list(hs)
        for k in range(steps_per_trip):
            rows = pl.ds(base + (steps_per_trip - 1 - k if reverse else k) * fold, fold)
            for s in range(n_seq):
                hs[s] = a_ref[s, rows, :] * hs[s] + u_ref[s, rows, :]
                hs_ref[s, rows, :] = hs[s]
        return tuple(hs)

    hs = lax.fori_loop(0, chunk // steps_per_trip, trip, tuple(state_ref[s] for s in range(n_seq)))
    for s in range(n_seq):
        state_ref[s] = hs[s]

    for s in range(n_seq):
        for c in range(fold):
            cols = slice(c * V7X_LANES, (c + 1) * V7X_LANES)
            scanned = hs_ref[s, folded_block(c), :]
            if reverse:
                out_ref[s, :, cols] = ((hf_ref[s, :, cols] + scanned)
                                       * gya_ref[s, :, cols].astype(F32)).astype(out_ref.dtype)
            else:
                out_ref[s, :, cols] = scanned

    @pl.when(j == n_chunks - 1)
    def _():
        for s in range(n_seq):
            last_ref[s] = hs[s]


def _lru_call(layer, xc, w_gate, ba, bx, lam, h0, seq_len, reverse, hf=None, gya=None):
    nb = xc.shape[0]
    chunk = min(LRU_CHUNK, seq_len)
    nt = seq_len // chunk
    n_seq = min(LRU_SEQS_PER_STEP, nb)
    fold = D_LRU // V7X_LANES
    pos = (lambda j: nt - 1 - j) if reverse else (lambda j: j)
    cur = lambda b, j: (b, pos(j), 0)
    per_seq = pl.BlockSpec((n_seq, fold, V7X_LANES), lambda b, j: (b, 0, 0))
    in_specs = [
        pl.BlockSpec((n_seq, chunk, D_LRU), cur),
        _resident((LRU_BLOCKS, LRU_BLOCK, 2 * LRU_BLOCK), lead=(layer, int(reverse))),
        _resident((1, D_LRU)),
        _resident((1, D_LRU)),
        _resident((1, D_LRU)),
        per_seq,
    ]
    args = [xc, w_gate, ba, bx, lam, h0]
    folded = pltpu.VMEM((n_seq, chunk * fold, V7X_LANES), F32)
    scratch = [folded, folded, folded, pltpu.VMEM((n_seq, fold, V7X_LANES), F32)]
    if reverse:
        in_specs += [pl.BlockSpec((n_seq, chunk, D_LRU), cur)] * 2
        args += [hf, gya]
    return pl.pallas_call(
        functools.partial(_lru_kernel, chunk=chunk, n_chunks=nt, n_seq=n_seq, reverse=reverse),
        grid=(nb // n_seq, nt),
        in_specs=in_specs,
        out_specs=[pl.BlockSpec((n_seq, chunk, D_LRU), cur), per_seq],
        out_shape=[jax.ShapeDtypeStruct((nb, seq_len, D_LRU), BF16 if reverse else F32),
                   jax.ShapeDtypeStruct((nb, fold, V7X_LANES), F32)],
        scratch_shapes=scratch,
        compiler_params=_params("arbitrary", "arbitrary"),
        name="lru_bwd" if reverse else "lru_fwd",
    )(*args)


def _attn_kernel(*refs, tq, seq_len, n_seq, mode, has_ctx, has_sink):
    it = iter(refs)
    q_ref, k_ref, v_ref = next(it), next(it), next(it)
    ck_ref = cv_ref = sink_ref = None
    if has_ctx:
        ck_ref, cv_ref = next(it), next(it)
    if has_sink:
        sink_ref = next(it)
    o_ref = next(it)
    m_ref, acc_ref = next(it), next(it)

    qi = pl.program_id(1)
    group = N_HEADS // N_KV
    acc_rows = HEAD_DIM + ATTN_ONES
    acc_row = lax.broadcasted_iota(jnp.int32, (acc_rows, tq), 0)
    lane = lax.broadcasted_iota(jnp.int32, (tq, V7X_LANES), 1)
    low = lane < HEAD_DIM
    key_tile = ATTN_KEY_TILE

    slots = [(s, h) for s in range(n_seq) for h in range(N_HEADS)]
    qh = []
    for slot, (s, h) in enumerate(slots):
        qp = q_ref[s * tq:(s + 1) * tq, (h // 2) * V7X_LANES:(h // 2 + 1) * V7X_LANES]
        zero = jnp.zeros_like(qp)
        qh.append(jnp.where(low, zero, qp) if h % 2 else jnp.where(low, qp, zero))
        if has_sink:
            m_ref[slot] = jnp.full((1, tq), sink_ref[h] * LOG2_E, F32)
            acc_ref[slot] = jnp.where(acc_row < HEAD_DIM, 0.0, 1.0)
        else:
            m_ref[slot] = jnp.full((1, tq), NEG_INF, F32)
            acc_ref[slot] = jnp.zeros((acc_rows, tq), F32)

    def scores(item):
        slot, key_ref, _, keys, _ = item
        g = slots[slot][1] // group
        k = key_ref[keys, g * V7X_LANES:(g + 1) * V7X_LANES]
        return lax.dot_general(k, qh[slot], (((1,), (1,)), ((), ())), preferred_element_type=F32)

    def absorb(item, st):
        slot, _, val_ref, keys, mask = item
        g = slots[slot][1] // group
        vt = val_ref[keys, g * V7X_LANES:(g + 1) * V7X_LANES].T[:acc_rows]
        if mask is not None:
            st = jnp.where(mask, st, NEG_INF)
        m_old = m_ref[slot]
        m_new = jnp.maximum(m_old, jnp.max(st, axis=0, keepdims=True))
        alpha = jnp.exp2(m_old - m_new)
        p = jnp.exp2(st - m_new)
        acc_ref[slot] = alpha * acc_ref[slot] + _dot(vt, p.astype(BF16))
        m_ref[slot] = m_new

    def run(items):
        pending = {}
        for i in range(len(items) + ATTN_LOOKAHEAD):
            if i < len(items):
                pending[i] = scores(items[i])
            j = i - ATTN_LOOKAHEAD
            if j >= 0:
                absorb(items[j], pending.pop(j))

    def tile_items(key_ref, val_ref, keys_of_seq, mask):
        return [(slot, key_ref, val_ref, keys_of_seq(s), mask) for slot, (s, _) in enumerate(slots)]

    head_items = tile_items(ck_ref, cv_ref, lambda s: slice(None), None) if has_ctx else []
    if mode == "full":
        n_tiles = seq_len // key_tile
        if n_tiles == 1:
            run(head_items + tile_items(k_ref, v_ref, lambda s: slice(s * seq_len, (s + 1) * seq_len), None))
        else:
            run(head_items)
            per_step = ATTN_TILES_PER_STEP

            def body(t, carry):
                items = []
                for u in range(per_step):
                    keys = pl.ds(pl.multiple_of((t * per_step + u) * key_tile, key_tile), key_tile)
                    items += tile_items(k_ref, v_ref, lambda s: keys, None)
                run(items)
                return carry
            lax.fori_loop(0, n_tiles // per_step, body, 0)
    else:
        span = tq + 2 * WINDOW
        q0 = qi * tq
        start = pl.multiple_of(jnp.clip(q0 - WINDOW, 0, seq_len - span), WINDOW)
        items = head_items
        for u in range(span // key_tile):
            k_abs = start + u * key_tile + lax.broadcasted_iota(jnp.int32, (key_tile, tq), 0)
            q_abs = q0 + lax.broadcasted_iota(jnp.int32, (key_tile, tq), 1)
            keys = pl.ds(start + u * key_tile, key_tile)
            items = items + tile_items(k_ref, v_ref, lambda s: keys, jnp.abs(q_abs - k_abs) <= WINDOW)
        run(items)

    def normalised(slot):
        acc = acc_ref[slot]
        return (acc[:HEAD_DIM] * (1.0 / acc[HEAD_DIM:HEAD_DIM + 1])).T

    for s in range(n_seq):
        for pair in range(N_HEADS // 2):
            halves = [normalised(s * N_HEADS + 2 * pair), normalised(s * N_HEADS + 2 * pair + 1)]
            o_ref[s * tq:(s + 1) * tq, pair * V7X_LANES:(pair + 1) * V7X_LANES] = (
                jnp.concatenate(halves, axis=1).astype(o_ref.dtype))


def _attn_call(q, kd, vd, seq_len, mode, ctx=None, sink=None):
    ntok = q.shape[0]
    nb = ntok // seq_len
    tq = min(ATTN_Q_TILE, seq_len)
    nq = seq_len // tq
    n_seq = ATTN_SEQS_PER_STEP if nq == 1 else 1
    group = N_HEADS // N_KV
    in_specs = [
        pl.BlockSpec((n_seq * tq, D_Q), lambda b, i: (b * nq + i, 0)),
        pl.BlockSpec((n_seq * seq_len, 2 * D_KV), lambda b, i: (b, 0)),
        pl.BlockSpec((n_seq * seq_len, 2 * D_KV), lambda b, i: (b, 0)),
    ]
    args = [q, kd, vd]
    if ctx is not None:
        assert n_seq == 1
        past = ctx[0].shape[1]
        in_specs += [pl.BlockSpec((None, past, 2 * D_KV), lambda b, i: (b, 0, 0))] * 2
        args += list(ctx)
    if sink is not None:
        in_specs.append(pl.BlockSpec(memory_space=pltpu.SMEM))
        args.append(sink)
    return pl.pallas_call(
        functools.partial(_attn_kernel, tq=tq, seq_len=seq_len, n_seq=n_seq, mode=mode,
                          has_ctx=ctx is not None, has_sink=sink is not None),
        grid=(nb // n_seq, nq),
        in_specs=in_specs,
        out_specs=pl.BlockSpec((n_seq * tq, D_Q), lambda b, i: (b * nq + i, 0)),
        out_shape=jax.ShapeDtypeStruct((ntok, D_Q), BF16),
        scratch_shapes=[
            pltpu.VMEM((n_seq * N_HEADS, 1, tq), F32),
            pltpu.VMEM((n_seq * N_HEADS, HEAD_DIM + ATTN_ONES, tq), F32),
        ],
        compiler_params=_params("parallel", "parallel"),
        name="attn_" + mode + ("_sink" if sink is not None else ""),
    )(*args)


def _mix_out_kernel(oa_ref, ob_ref, oc_ref, x_ref, mod_ref, g_ref, wgate_ref,
                    woa_ref, wob_ref, woc_ref, wout_ref, o_ref):
    mod = mod_ref[0]
    x = x_ref[...]
    h = (_rms_rows(x) * (g_ref[...] * (1.0 + mod[1:2])) + mod[0:1]).astype(BF16)

    def branch(k, o_ref_k, w_ref_k):
        gate = _sigmoid(_dot(h, wgate_ref[:, k * D_MODEL:(k + 1) * D_MODEL]))
        return gate * _dot(o_ref_k[...], w_ref_k[...])

    merged = branch(0, oa_ref, woa_ref) + branch(1, ob_ref, wob_ref) + branch(2, oc_ref, woc_ref)
    out = _dot(merged.astype(BF16), wout_ref[...])
    o_ref[...] = x + mod[2:3] * out


def _mix_out_call(layer, oa, ob, oc, x, mod, g, w_gate, w_oa, w_ob, w_oc, w_out, seq_len):
    ntok = x.shape[0]
    tm = TOKEN_TILE
    tiles_per_seq = max(seq_len // tm, 1)
    cond_of = (lambda i: i // tiles_per_seq) if mod.shape[0] > 1 else (lambda i: 0)
    row = lambda w: pl.BlockSpec((tm, w), lambda i: (i, 0))
    return pl.pallas_call(
        _mix_out_kernel,
        grid=(ntok // tm,),
        in_specs=[row(D_LRU), row(D_Q), row(D_Q), row(D_MODEL),
                  pl.BlockSpec((1, 6, D_MODEL), lambda i: (cond_of(i), 0, 0)),
                  _resident((1, D_MODEL)), _resident((D_MODEL, D_IN - D_TOKEN_MIX), lead=(layer,)),
                  _resident((D_LRU, D_MODEL), lead=(layer,)), _resident((D_Q, D_MODEL), lead=(layer,)),
                  _resident((D_Q, D_MODEL), lead=(layer,)), _resident((D_MODEL, D_MODEL), lead=(layer,))],
        out_specs=row(D_MODEL),
        out_shape=jax.ShapeDtypeStruct((ntok, D_MODEL), F32),
        compiler_params=_params("parallel"),
        name="mix_out",
    )(oa, ob, oc, x, mod, g, w_gate, w_oa, w_ob, w_oc, w_out)


def _ffn_kernel(*refs, tm, seq_len, final):
    span_tiles = seq_len > tm
    it = iter(refs)
    x_ref = next(it)
    prev_ref, next_ref = (next(it), next(it)) if span_tiles else (None, None)
    mod_ref, g_ref, wup_ref, cw_ref, cb_ref, wdn_ref = (next(it) for _ in range(6))
    fg_ref = next(it) if final else None
    o_ref = next(it)
    h_ref, act_ref = next(it), next(it)

    halo = V7X_SUBLANES_F32
    sub = min(tm, seq_len)
    mod = mod_ref[0]
    shift, scale, gate2 = mod[3:4], mod[4:5], mod[5:6]
    gain = g_ref[...] * (1.0 + scale)
    x = x_ref[...]

    def modulated(v):
        return (_rms_rows(v) * gain + shift).astype(BF16)

    if span_tiles:
        tile_in_seq = pl.program_id(0) % (seq_len // tm)
        at_start, at_end = tile_in_seq == 0, tile_in_seq == seq_len // tm - 1
        h_ref[0:halo, :] = modulated(prev_ref[...])
        h_ref[halo:halo + tm, :] = modulated(x)
        h_ref[halo + tm:, :] = modulated(next_ref[...])
        centre = slice(halo, halo + tm)
    else:
        h_ref[...] = modulated(x)
        centre = slice(None)

    cw = cw_ref[...]
    zero_halo = jnp.zeros((halo, FFN_CHUNK), F32)
    for c in range(D_FF // FFN_CHUNK):
        cols = slice(c * FFN_CHUNK, (c + 1) * FFN_CHUNK)
        vcols = slice(D_FF + c * FFN_CHUNK, D_FF + (c + 1) * FFN_CHUNK)
        pre = _dot(h_ref[...], wup_ref[:, cols])
        val = _dot(h_ref[centre, :], wup_ref[:, vcols])
        parts = []
        for b in range(tm // sub):
            if span_tiles:
                ext = jnp.concatenate([jnp.where(at_start, 0.0, pre[:halo]), pre[halo:halo + tm],
                                       jnp.where(at_end, 0.0, pre[halo + tm:])], axis=0)
            else:
                ext = jnp.concatenate([zero_halo, pre[b * sub:(b + 1) * sub], zero_halo], axis=0)
            before = pltpu.roll(ext, 1, 0)[halo:halo + sub]
            after = pltpu.roll(ext, sub + 2 * halo - 1, 0)[halo:halo + sub]
            parts.append(cb_ref[:, cols] + cw[0:1, cols] * before + cw[1:2, cols] * ext[halo:halo + sub]
                         + cw[2:3, cols] * after)
        gate = parts[0] if len(parts) == 1 else jnp.concatenate(parts, axis=0)
        act_ref[:, cols] = (_gelu_tanh(gate) * val).astype(BF16)

    y = x + gate2 * _dot(act_ref[...], wdn_ref[...])
    if final:
        y = _rms_rows(y) * fg_ref[...]
    o_ref[...] = y


def _ffn_call(layer, x, mod, g, w_up, conv_w, conv_b, w_down, seq_len, final_g=None):
    ntok = x.shape[0]
    tm = TOKEN_TILE
    halo = V7X_SUBLANES_F32
    tiles_per_seq = max(seq_len // tm, 1)
    cond_of = (lambda i: i // tiles_per_seq) if mod.shape[0] > 1 else (lambda i: 0)
    n_halo_blocks = ntok // halo
    span_tiles = seq_len > tm
    in_specs = [pl.BlockSpec((tm, D_MODEL), lambda i: (i, 0))]
    args = [x]
    if span_tiles:
        in_specs += [
            pl.BlockSpec((halo, D_MODEL), lambda i: (jnp.maximum(i * (tm // halo) - 1, 0), 0)),
            pl.BlockSpec((halo, D_MODEL), lambda i: (jnp.minimum((i + 1) * (tm // halo), n_halo_blocks - 1), 0)),
        ]
        args += [x, x]
    in_specs += [
        pl.BlockSpec((1, 6, D_MODEL), lambda i: (cond_of(i), 0, 0)),
        _resident((1, D_MODEL)),
        _resident((D_MODEL, 2 * D_FF), lead=(layer,)),
        _resident((3, D_FF)),
        _resident((1, D_FF)),
        _resident((D_FF, D_MODEL), lead=(layer,)),
    ]
    args += [mod, g, w_up, conv_w, conv_b, w_down]
    h_rows = tm + 2 * halo if span_tiles else tm
    final = final_g is not None
    if final:
        in_specs.append(_resident((1, D_MODEL)))
        args.append(final_g)
    return pl.pallas_call(
        functools.partial(_ffn_kernel, tm=tm, seq_len=seq_len, final=final),
        grid=(ntok // tm,),
        in_specs=in_specs,
        out_specs=pl.BlockSpec((tm, D_MODEL), lambda i: (i, 0)),
        out_shape=jax.ShapeDtypeStruct((ntok, D_MODEL), F32),
        scratch_shapes=[
            pltpu.VMEM((h_rows, D_MODEL), BF16),
            pltpu.VMEM((tm, D_FF), BF16),
        ],
        compiler_params=_params("parallel"),
        name="ffn",
    )(*args)


def _rope_tables(seq_len):
    nf = HEAD_DIM // 4
    n_rows = seq_len // GRID_W
    inv = ROPE_THETA ** (-jnp.arange(nf, dtype=F32) / nf)
    ang_r = jnp.arange(n_rows, dtype=F32)[:, None] * inv[None, :]
    ang_c = jnp.arange(GRID_W, dtype=F32)[:, None] * inv[None, :]
    by_row = lambda a: jnp.repeat(a, GRID_W, axis=0)
    by_col = lambda a: jnp.tile(a, (n_rows, 1))
    cos_r, sin_r, cos_c, sin_c = by_row(jnp.cos(ang_r)), by_row(jnp.sin(ang_r)), by_col(jnp.cos(ang_c)), by_col(jnp.sin(ang_c))
    cos = jnp.concatenate([cos_r, cos_r, cos_c, cos_c], axis=1)
    sin = jnp.concatenate([-sin_r, sin_r, -sin_c, sin_c], axis=1)
    reps = V7X_LANES // HEAD_DIM
    return jnp.concatenate([cos] * reps, axis=1), jnp.concatenate([sin] * reps, axis=1)


def _segment_mean_matrix(width):
    seg = np.arange(width) // HEAD_DIM
    return jnp.asarray((seg[:, None] == seg[None, :]).astype(np.float32) / HEAD_DIM, dtype=BF16)


def _dup_cache(c):
    return jnp.concatenate([c[:, :, 0], c[:, :, 0], c[:, :, 1], c[:, :, 1]], axis=-1).astype(BF16)


def _ext_cache(c):
    b, n = c.shape[:2]
    ones = jnp.ones((b, n, ATTN_ONES), c.dtype)
    zeros = jnp.zeros((b, n, V7X_LANES - HEAD_DIM - ATTN_ONES), c.dtype)
    return jnp.concatenate([c[:, :, 0], ones, zeros, c[:, :, 1], ones, zeros], axis=-1).astype(BF16)


def kernel(x_prompt, x_sample, c, cache_kb, cache_vb, cache_kc, cache_vc, state_lru, c_ctx, norm1_g, norm2_g, w_mod, b_mod, w_in, lru_conv_w, lru_conv_b, lru_wa, lru_ba, lru_wx, lru_bx, lru_lam, qnorm_g, knorm_g, sink_c, w_oa, w_ob, w_oc, w_out, w_up, ffn_conv_w, ffn_conv_b, w_down, final_g):
    batch, seq, d = x_prompt.shape
    dec_batch, dec_seq, _ = x_sample.shape
    depth = w_in.shape[0]
    assert d == D_MODEL and depth == DEPTH and dec_batch + 1 <= 8

    cond8 = jnp.zeros((8, d), F32).at[0].set(c_ctx).at[1:1 + dec_batch].set(c)
    mod_all = _mod_call(cond8, w_mod, b_mod).reshape(depth, 8, 6, d)

    w_in_b = w_in[:, :, :D_TOKEN_MIX].astype(BF16)
    w_bgate_b = w_in[:, :, D_TOKEN_MIX:].astype(BF16)
    w_oa_b, w_ob_b, w_oc_b = (w.astype(BF16) for w in (w_oa, w_ob, w_oc))
    w_out_b, w_up_b, w_down_b = (w.astype(BF16) for w in (w_out, w_up, w_down))
    w_gate = (0.5 * jnp.concatenate([lru_wa, lru_wx], axis=-1)).astype(BF16)
    half_ba, half_bx = 0.5 * lru_ba, 0.5 * lru_bx
    seg_q, seg_k = _segment_mean_matrix(D_Q), _segment_mean_matrix(D_KV)
    rope_tabs = _rope_tables(dec_seq)
    qg = jnp.tile(qnorm_g, (1, N_HEADS))
    kg = jnp.tile(knorm_g, (1, N_KV))

    def layer(l, x, mod, seq_len, h0, ctx, is_last):
        is_ctx = ctx is None
        ntok = x.shape[0]
        outs = _in_proj_call(l, x, mod, norm1_g[l][None], w_in_b, qg[l][None], kg[l][None], seg_q, seg_k,
                             lru_conv_w[l], lru_conv_b[l][None], None if is_ctx else rope_tabs, seq_len,
                             emit_kv=is_ctx)
        xc, gya, qb, kbd, vbd, qc, kcd, vcd = outs[:8]
        per_seq = lambda a: a.reshape(ntok // seq_len, seq_len, D_LRU)
        lru_args = lambda dr: (w_gate, half_ba[l, dr][None], half_bx[l, dr][None], lru_lam[l, dr][None],
                               h0[:, dr].reshape(-1, D_LRU // V7X_LANES, V7X_LANES), seq_len)
        hf, last_f = _lru_call(l, per_seq(xc), *lru_args(0), reverse=False)
        oa, last_b = _lru_call(l, per_seq(xc), *lru_args(1), reverse=True, hf=hf, gya=per_seq(gya))
        oa = oa.reshape(ntok, D_LRU)
        if is_ctx:
            ob = _attn_call(qb, kbd, vbd, seq_len, "full")
            oc = _attn_call(qc, kcd, vcd, seq_len, "full", sink=sink_c[l])
        else:
            ob = _attn_call(qb, kbd, vbd, seq_len, "full", ctx=(ctx["kb"], ctx["vb"]))
            oc = _attn_call(qc, kcd, vcd, seq_len, "window", ctx=(ctx["kc"], ctx["vc"]), sink=sink_c[l])
        x = _mix_out_call(l, oa, ob, oc, x, mod, norm1_g[l][None], w_bgate_b, w_oa_b, w_ob_b, w_oc_b, w_out_b,
                          seq_len)
        x = _ffn_call(l, x, mod, norm2_g[l][None], w_up_b, ffn_conv_w[l], ffn_conv_b[l][None], w_down_b,
                      seq_len, final_g=final_g[None] if is_last else None)
        return x, outs[8:], (last_f, last_b)

    xp = x_prompt.reshape(batch * seq, d)
    xs = x_sample.reshape(dec_batch * dec_seq, d)
    zeros_h0 = jnp.zeros((batch, 2, D_LRU), F32)
    kbs, vbs, kcs, vcs, lrus = [], [], [], [], []
    for l in range(depth):
        is_last = l == depth - 1
        xp, (kb, vb, kc, vc), (last_f, last_b) = layer(l, xp, mod_all[l, 0:1], seq, zeros_h0, None, is_last)
        kbs.append(kb)
        vbs.append(vb)
        kcs.append(kc)
        vcs.append(vc)
        lrus.append(jnp.stack([last_f.reshape(batch, D_LRU), last_b.reshape(batch, D_LRU)], axis=1))
        cached = {"kb": _dup_cache(cache_kb[:, l]), "vb": _ext_cache(cache_vb[:, l]),
                  "kc": _dup_cache(cache_kc[:, l]), "vc": _ext_cache(cache_vc[:, l])}
        xs, _, _ = layer(l, xs, mod_all[l, 1:1 + dec_batch], dec_seq, state_lru[:, l], cached, is_last)

    def stack_kv(parts):
        return jnp.stack([p.reshape(batch, seq, N_KV, HEAD_DIM) for p in parts], axis=1)

    y_prompt = xp.reshape(batch, seq, d)
    y_sample = xs.reshape(dec_batch, dec_seq, d)
    return (y_prompt, y_sample, stack_kv(kbs), stack_kv(vbs), stack_kv(kcs), stack_kv(vcs),
            jnp.stack(lrus, axis=1))
```

```python
import functools

import jax
import jax.numpy as jnp
import numpy as np
from jax import lax
from jax.experimental import pallas as pl
from jax.experimental.pallas import tpu as pltpu

D_MODEL = 1024
DEPTH = 2
GRID_W = 64
D_LRU = 1024
LRU_BLOCKS = 8
LRU_BLOCK = D_LRU // LRU_BLOCKS
LRU_C = 8.0
HEAD_DIM = 64
N_HEADS = 8
N_KV = 2
D_Q = N_HEADS * HEAD_DIM
D_KV = N_KV * HEAD_DIM
WINDOW = 128
D_FF = 2816
ROPE_THETA = 10000.0
NORM_EPS = 1e-6
NEG_INF = -1e30
ATTN_SCALE = HEAD_DIM ** -0.5
LOG2_E = 1.4426950408889634
LN_2 = 0.6931471805599453
IN_SIZES = (D_LRU, D_LRU, D_Q, D_KV, D_KV, D_Q, D_KV, D_KV, D_MODEL, D_MODEL, D_MODEL)
IN_OFF = tuple(int(v) for v in np.cumsum((0,) + IN_SIZES))
D_IN = IN_OFF[-1]
D_TOKEN_MIX = IN_OFF[8]

V7X_LANES = 128
V7X_SUBLANES_F32 = 8
V7X_VMEM_LIMIT_BYTES = 56 * 1024 * 1024

TOKEN_TILE = 512
WIDE_TOKEN_TILE = 1024
FFN_CHUNK = 256
ATTN_Q_TILE = 256
ATTN_KEY_TILE = 256
ATTN_SEQS_PER_STEP = 8
ATTN_TILES_PER_STEP = 4
ATTN_LOOKAHEAD = 5
ATTN_ONES = 16
LRU_CHUNK = 512
LRU_SEQS_PER_STEP = 4
LRU_SCAN_UNROLL = 8
MOD_COL_TILE = 1536

F32 = jnp.float32
BF16 = jnp.bfloat16


def _params(*sem):
    return pltpu.CompilerParams(dimension_semantics=sem, vmem_limit_bytes=V7X_VMEM_LIMIT_BYTES)


def _resident(shape, lead=()):
    nd = len(shape)
    lead = tuple(int(i) for i in lead)
    return pl.BlockSpec((None,) * len(lead) + tuple(shape), lambda *_: lead + (0,) * nd,
                        pipeline_mode=pl.Buffered(1))


def _sigmoid(x):
    return 1.0 / (1.0 + jnp.exp(-x))


GELU_C0 = 0.7978845608028654
GELU_C1 = 0.044715 * GELU_C0


def _gelu_tanh(x):
    half = 0.5 * x
    return half + half * jnp.tanh(x * (GELU_C0 + GELU_C1 * (x * x)))


def _rms_rows(x):
    return x * lax.rsqrt(jnp.mean(x * x, axis=-1, keepdims=True) + NORM_EPS)


def _dot(a, b):
    return jnp.dot(a, b, preferred_element_type=F32)


def _mod_kernel(c_ref, w_ref, b_ref, o_ref):
    c = c_ref[...]
    s = (c * _sigmoid(c)).astype(BF16)
    o_ref[0] = _dot(s, w_ref[0].astype(BF16)) + b_ref[0]


def _mod_call(cond8, w_mod, b_mod):
    depth, d, n = w_mod.shape
    tn = MOD_COL_TILE
    return pl.pallas_call(
        _mod_kernel,
        grid=(depth, n // tn),
        in_specs=[
            pl.BlockSpec((8, d), lambda l, j: (0, 0)),
            pl.BlockSpec((1, d, tn), lambda l, j: (l, 0, j)),
            pl.BlockSpec((1, 1, tn), lambda l, j: (l, 0, j)),
        ],
        out_specs=pl.BlockSpec((1, 8, tn), lambda l, j: (l, 0, j)),
        out_shape=jax.ShapeDtypeStruct((depth, 8, n), F32),
        compiler_params=_params("parallel", "parallel"),
        name="mod",
    )(cond8, w_mod, b_mod.reshape(depth, 1, n))


def _head_rms(x, seg_ref, g):
    sq = x * x
    hi = sq.astype(BF16)
    lo = (sq - hi.astype(F32)).astype(BF16)
    ms = _dot(hi, seg_ref[...]) + _dot(lo, seg_ref[...])
    return x * lax.rsqrt(ms + NORM_EPS) * g


def _rope(x, cos, sin):
    w = x.shape[1]
    lane = lax.broadcasted_iota(jnp.int32, x.shape, 1)
    fwd = pltpu.roll(x, w - 16, 1)
    bwd = pltpu.roll(x, 16, 1)
    partner = jnp.where((lane % 32) < 16, fwd, bwd)
    return x * cos + partner * sin


def _dup_heads(x):
    lane = lax.broadcasted_iota(jnp.int32, x.shape, 1)
    swapped = pltpu.roll(x, HEAD_DIM, 1)
    low = lane < HEAD_DIM
    return jnp.concatenate([jnp.where(low, x, swapped), jnp.where(low, swapped, x)], axis=1)


def _ext_values(x):
    lane = lax.broadcasted_iota(jnp.int32, x.shape, 1)
    swapped = pltpu.roll(x, HEAD_DIM, 1)
    low = lane < HEAD_DIM
    tail = jnp.where(lane < HEAD_DIM + ATTN_ONES, 1.0, 0.0)
    return jnp.concatenate([jnp.where(low, x, tail), jnp.where(low, swapped, tail)], axis=1)


def _lru_conv(ext, cw, cb, n):
    halo = V7X_SUBLANES_F32
    taps = (pltpu.roll(ext, 2, 0), pltpu.roll(ext, 1, 0), ext, pltpu.roll(ext, n + 2 * halo - 1, 0))
    out = cb + cw[0:1] * taps[0][halo:halo + n]
    for k in range(1, 4):
        out = out + cw[k:k + 1] * taps[k][halo:halo + n]
    return out


def _in_proj_kernel(*refs, tm, seq_len, rope, emit_kv):
    span_tiles = seq_len > tm
    it = iter(refs)
    x_ref = next(it)
    prev_ref, next_ref = (next(it), next(it)) if span_tiles else (None, None)
    mod_ref, g_ref, w_ref, qg_ref, kg_ref, seg_q_ref, seg_k_ref, cw_ref, cb_ref = (next(it) for _ in range(9))
    cos_ref = sin_ref = None
    if rope:
        cos_ref, sin_ref = next(it), next(it)
    xc_ref, gya_ref, qb_ref, kbd_ref, vbd_ref, qc_ref, kcd_ref, vcd_ref = (next(it) for _ in range(8))
    if emit_kv:
        kb_ref, vb_ref, kc_ref, vc_ref = (next(it) for _ in range(4))

    mod = mod_ref[0]
    shift, scale = mod[0:1], mod[1:2]
    gain = g_ref[...] * (1.0 + scale)

    def modulated(v):
        return (_rms_rows(v) * gain + shift).astype(BF16)

    h = modulated(x_ref[...])

    def seg(lo, hi):
        return _dot(h, w_ref[:, IN_OFF[lo]:IN_OFF[hi]])

    halo = V7X_SUBLANES_F32
    qb_raw = seg(2, 3)
    kv_b = seg(3, 5)
    qkv_c = seg(5, 8)
    xa = seg(0, 1)
    if span_tiles:
        h_halo = modulated(jnp.concatenate([prev_ref[...], next_ref[...]], axis=0))
        xa_halo = _dot(h_halo, w_ref[:, IN_OFF[0]:IN_OFF[1]])
    qb = _head_rms(qb_raw, seg_q_ref, qg_ref[...])
    kb = _head_rms(kv_b[:, :D_KV], seg_k_ref, kg_ref[...])
    vb = kv_b[:, D_KV:]
    qc = qkv_c[:, :D_Q]
    kc = qkv_c[:, D_Q:D_Q + D_KV]
    vc = qkv_c[:, D_Q + D_KV:]
    gya_ref[...] = _gelu_tanh(seg(1, 2)).astype(BF16)

    cw, cb = cw_ref[...], cb_ref[...]
    if span_tiles:
        tile_in_seq = pl.program_id(0) % (seq_len // tm)
        ext = jnp.concatenate([jnp.where(tile_in_seq == 0, 0.0, xa_halo[:halo]), xa,
                               jnp.where(tile_in_seq == seq_len // tm - 1, 0.0, xa_halo[halo:])], axis=0)
        xc_ref[...] = _lru_conv(ext, cw, cb, tm)
    else:
        zero_halo = jnp.zeros((halo, D_LRU), F32)
        for b in range(tm // seq_len):
            rows = slice(b * seq_len, (b + 1) * seq_len)
            ext = jnp.concatenate([zero_halo, xa[rows], zero_halo], axis=0)
            xc_ref[rows, :] = _lru_conv(ext, cw, cb, seq_len)
    if emit_kv:
        kb_ref[...] = kb
        vb_ref[...] = vb
        kc_ref[...] = kc
        vc_ref[...] = vc
    if rope:
        cos, sin = cos_ref[...], sin_ref[...]
        cos_q = jnp.concatenate([cos] * (D_Q // V7X_LANES), axis=1)
        sin_q = jnp.concatenate([sin] * (D_Q // V7X_LANES), axis=1)
        qb = _rope(qb, cos_q, sin_q)
        qc = _rope(qc, cos_q, sin_q)
        kb = _rope(kb, cos, sin)
        kc = _rope(kc, cos, sin)
    qb_ref[...] = (qb * (ATTN_SCALE * LOG2_E)).astype(BF16)
    qc_ref[...] = (qc * (ATTN_SCALE * LOG2_E)).astype(BF16)
    kbd_ref[...] = _dup_heads(kb).astype(BF16)
    vbd_ref[...] = _ext_values(vb).astype(BF16)
    kcd_ref[...] = _dup_heads(kc).astype(BF16)
    vcd_ref[...] = _ext_values(vc).astype(BF16)


def _in_proj_call(layer, x, mod, g, w_in, qg, kg, seg_q, seg_k, conv_w, conv_b, rope_tabs, seq_len, emit_kv):
    ntok = x.shape[0]
    tm = TOKEN_TILE
    halo = V7X_SUBLANES_F32
    tiles_per_seq = max(seq_len // tm, 1)
    ncond = mod.shape[0]
    cond_of = (lambda i: i // tiles_per_seq) if ncond > 1 else (lambda i: 0)
    row = lambda w: pl.BlockSpec((tm, w), lambda i: (i, 0))
    in_specs = [row(D_MODEL)]
    args = [x]
    if seq_len > tm:
        n_halo_blocks = ntok // halo
        in_specs += [
            pl.BlockSpec((halo, D_MODEL), lambda i: (jnp.maximum(i * (tm // halo) - 1, 0), 0)),
            pl.BlockSpec((halo, D_MODEL), lambda i: (jnp.minimum((i + 1) * (tm // halo), n_halo_blocks - 1), 0)),
        ]
        args += [x, x]
    in_specs += [
        pl.BlockSpec((1, 6, D_MODEL), lambda i: (cond_of(i), 0, 0)),
        _resident((1, D_MODEL)),
        _resident((D_MODEL, D_TOKEN_MIX), lead=(layer,)),
        _resident((1, D_Q)),
        _resident((1, D_KV)),
        _resident((D_Q, D_Q)),
        _resident((D_KV, D_KV)),
        _resident((4, D_LRU)),
        _resident((1, D_LRU)),
    ]
    args += [mod, g, w_in, qg, kg, seg_q, seg_k, conv_w, conv_b]
    rope = rope_tabs is not None
    if rope:
        in_specs += [pl.BlockSpec((tm, V7X_LANES), lambda i: (i % tiles_per_seq, 0))] * 2
        args += list(rope_tabs)
    widths = [D_LRU, D_LRU, D_Q, 2 * D_KV, 2 * D_KV, D_Q, 2 * D_KV, 2 * D_KV]
    out_specs = [row(w) for w in widths]
    out_shape = [jax.ShapeDtypeStruct((ntok, w), F32 if n == 0 else BF16) for n, w in enumerate(widths)]
    if emit_kv:
        out_specs += [row(D_KV)] * 4
        out_shape += [jax.ShapeDtypeStruct((ntok, D_KV), F32)] * 4
    return pl.pallas_call(
        functools.partial(_in_proj_kernel, tm=tm, seq_len=seq_len, rope=rope, emit_kv=emit_kv),
        grid=(ntok // tm,),
        in_specs=in_specs,
        out_specs=out_specs,
        out_shape=out_shape,
        compiler_params=_params("parallel"),
        name="in_proj",
    )(*args)


def _lru_kernel(*refs, chunk, n_chunks, n_seq, reverse):
    it = iter(refs)
    xc_ref, wg_ref, ba_ref, bx_ref, lam_ref, h0_ref = (next(it) for _ in range(6))
    if reverse:
        hf_ref, gya_ref = next(it), next(it)
    out_ref, last_ref = next(it), next(it)
    a_ref, u_ref, hs_ref, state_ref = next(it), next(it), next(it), next(it)

    fold = D_LRU // V7X_LANES

    def folded_block(c):
        return pl.ds(c, chunk, stride=fold)

    j = pl.program_id(1)
    lam = lam_ref[...]
    softplus_neg_lam = jnp.maximum(-lam, 0.0) + jnp.log1p(jnp.exp(-jnp.abs(lam)))
    slope = (-0.5 * LRU_C * LOG2_E) * softplus_neg_lam

    for s in range(n_seq):
        xc = xc_ref[s]
        xc_b = xc.astype(BF16)
        half_xc = 0.5 * xc

        for n in range(LRU_BLOCKS):
            cols = slice(n * LRU_BLOCK, (n + 1) * LRU_BLOCK)
            z = _dot(xc_b[:, cols], wg_ref[n])
            t_r = jnp.tanh(z[:, :LRU_BLOCK] + ba_ref[:, cols])
            t_i = jnp.tanh(z[:, LRU_BLOCK:] + bx_ref[:, cols])
            log2_a = slope[:, cols] + slope[:, cols] * t_r
            a = jnp.exp2(log2_a)
            one_minus_a2 = jnp.tanh(log2_a * (-LN_2)) * (a * a + 1.0)
            root = jnp.where(one_minus_a2 > 0.0, one_minus_a2 * lax.rsqrt(one_minus_a2), 0.0)
            hx = half_xc[:, cols]
            a_ref[s, folded_block(n), :] = a
            u_ref[s, folded_block(n), :] = root * (hx + hx * t_i)

    @pl.when(j == 0)
    def _():
        state_ref[...] = h0_ref[...]

    steps_per_trip = LRU_SCAN_UNROLL

    def trip(gi, hs):
        gidx = (chunk // steps_per_trip - 1 - gi) if reverse else gi
        base = pl.multiple_of(gidx * (steps_per_trip * fold), steps_per_trip * fold)
        hs = list(hs)
        for k in range(steps_per_trip):
            rows = pl.ds(base + (steps_per_trip - 1 - k if reverse else k) * fold, fold)
            for s in range(n_seq):
                hs[s] = a_ref[s, rows, :] * hs[s] + u_ref[s, rows, :]
                hs_ref[s, rows, :] = hs[s]
        return tuple(hs)

    hs = lax.fori_loop(0, chunk // steps_per_trip, trip, tuple(state_ref[s] for s in range(n_seq)))
    for s in range(n_seq):
        state_ref[s] = hs[s]

    for s in range(n_seq):
        for c in range(fold):
            cols = slice(c * V7X_LANES, (c + 1) * V7X_LANES)
            scanned = hs_ref[s, folded_block(c), :]
            if reverse:
                out_ref[s, :, cols] = ((hf_ref[s, :, cols] + scanned)
                                       * gya_ref[s, :, cols].astype(F32)).astype(out_ref.dtype)
            else:
                out_ref[s, :, cols] = scanned

    @pl.when(j == n_chunks - 1)
    def _():
        for s in range(n_seq):
            last_ref[s] = hs[s]


def _lru_call(layer, xc, w_gate, ba, bx, lam, h0, seq_len, reverse, hf=None, gya=None):
    nb = xc.shape[0]
    chunk = min(LRU_CHUNK, seq_len)
    nt = seq_len // chunk
    n_seq = min(LRU_SEQS_PER_STEP, nb)
    fold = D_LRU // V7X_LANES
    pos = (lambda j: nt - 1 - j) if reverse else (lambda j: j)
    cur = lambda b, j: (b, pos(j), 0)
    per_seq = pl.BlockSpec((n_seq, fold, V7X_LANES), lambda b, j: (b, 0, 0))
    in_specs = [
        pl.BlockSpec((n_seq, chunk, D_LRU), cur),
        _resident((LRU_BLOCKS, LRU_BLOCK, 2 * LRU_BLOCK), lead=(layer, int(reverse))),
        _resident((1, D_LRU)),
        _resident((1, D_LRU)),
        _resident((1, D_LRU)),
        per_seq,
    ]
    args = [xc, w_gate, ba, bx, lam, h0]
    folded = pltpu.VMEM((n_seq, chunk * fold, V7X_LANES), F32)
    scratch = [folded, folded, folded, pltpu.VMEM((n_seq, fold, V7X_LANES), F32)]
    if reverse:
        in_specs += [pl.BlockSpec((n_seq, chunk, D_LRU), cur)] * 2
        args += [hf, gya]
    return pl.pallas_call(
        functools.partial(_lru_kernel, chunk=chunk, n_chunks=nt, n_seq=n_seq, reverse=reverse),
        grid=(nb // n_seq, nt),
        in_specs=in_specs,
        out_specs=[pl.BlockSpec((n_seq, chunk, D_LRU), cur), per_seq],
        out_shape=[jax.ShapeDtypeStruct((nb, seq_len, D_LRU), BF16 if reverse else F32),
                   jax.ShapeDtypeStruct((nb, fold, V7X_LANES), F32)],
        scratch_shapes=scratch,
        compiler_params=_params("arbitrary", "arbitrary"),
        name="lru_bwd" if reverse else "lru_fwd",
    )(*args)


def _attn_kernel(*refs, tq, seq_len, n_seq, mode, has_ctx, has_sink):
    it = iter(refs)
    q_ref, k_ref, v_ref = next(it), next(it), next(it)
    ck_ref = cv_ref = sink_ref = None
    if has_ctx:
        ck_ref, cv_ref = next(it), next(it)
    if has_sink:
        sink_ref = next(it)
    o_ref = next(it)
    m_ref, acc_ref = next(it), next(it)

    qi = pl.program_id(1)
    group = N_HEADS // N_KV
    acc_rows = HEAD_DIM + ATTN_ONES
    acc_row = lax.broadcasted_iota(jnp.int32, (acc_rows, tq), 0)
    lane = lax.broadcasted_iota(jnp.int32, (tq, V7X_LANES), 1)
    low = lane < HEAD_DIM
    key_tile = ATTN_KEY_TILE

    slots = [(s, h) for s in range(n_seq) for h in range(N_HEADS)]
    qh = []
    for slot, (s, h) in enumerate(slots):
        qp = q_ref[s * tq:(s + 1) * tq, (h // 2) * V7X_LANES:(h // 2 + 1) * V7X_LANES]
        zero = jnp.zeros_like(qp)
        qh.append(jnp.where(low, zero, qp) if h % 2 else jnp.where(low, qp, zero))
        if has_sink:
            m_ref[slot] = jnp.full((1, tq), sink_ref[h] * LOG2_E, F32)
            acc_ref[slot] = jnp.where(acc_row < HEAD_DIM, 0.0, 1.0)
        else:
            m_ref[slot] = jnp.full((1, tq), NEG_INF, F32)
            acc_ref[slot] = jnp.zeros((acc_rows, tq), F32)

    def scores(item):
        slot, key_ref, _, keys, _ = item
        g = slots[slot][1] // group
        k = key_ref[keys, g * V7X_LANES:(g + 1) * V7X_LANES]
        return lax.dot_general(k, qh[slot], (((1,), (1,)), ((), ())), preferred_element_type=F32)

    def absorb(item, st):
        slot, _, val_ref, keys, mask = item
        g = slots[slot][1] // group
        vt = val_ref[keys, g * V7X_LANES:(g + 1) * V7X_LANES].T[:acc_rows]
        if mask is not None:
            st = jnp.where(mask, st, NEG_INF)
        m_old = m_ref[slot]
        m_new = jnp.maximum(m_old, jnp.max(st, axis=0, keepdims=True))
        alpha = jnp.exp2(m_old - m_new)
        p = jnp.exp2(st - m_new)
        acc_ref[slot] = alpha * acc_ref[slot] + _dot(vt, p.astype(BF16))
        m_ref[slot] = m_new

    def run(items):
        pending = {}
        for i in range(len(items) + ATTN_LOOKAHEAD):
            if i < len(items):
                pending[i] = scores(items[i])
            j = i - ATTN_LOOKAHEAD
            if j >= 0:
                absorb(items[j], pending.pop(j))

    def tile_items(key_ref, val_ref, keys_of_seq, mask):
        return [(slot, key_ref, val_ref, keys_of_seq(s), mask) for slot, (s, _) in enumerate(slots)]

    head_items = tile_items(ck_ref, cv_ref, lambda s: slice(None), None) if has_ctx else []
    if mode == "full":
        n_tiles = seq_len // key_tile
        if n_tiles == 1:
            run(head_items + tile_items(k_ref, v_ref, lambda s: slice(s * seq_len, (s + 1) * seq_len), None))
        else:
            run(head_items)
            per_step = ATTN_TILES_PER_STEP

            def body(t, carry):
                items = []
                for u in range(per_step):
                    keys = pl.ds(pl.multiple_of((t * per_step + u) * key_tile, key_tile), key_tile)
                    items += tile_items(k_ref, v_ref, lambda s: keys, None)
                run(items)
                return carry
            lax.fori_loop(0, n_tiles // per_step, body, 0)
    else:
        span = tq + 2 * WINDOW
        q0 = qi * tq
        start = pl.multiple_of(jnp.clip(q0 - WINDOW, 0, seq_len - span), WINDOW)
        items = head_items
        for u in range(span // key_tile):
            k_abs = start + u * key_tile + lax.broadcasted_iota(jnp.int32, (key_tile, tq), 0)
            q_abs = q0 + lax.broadcasted_iota(jnp.int32, (key_tile, tq), 1)
            keys = pl.ds(start + u * key_tile, key_tile)
            items = items + tile_items(k_ref, v_ref, lambda s: keys, jnp.abs(q_abs - k_abs) <= WINDOW)
        run(items)

    def normalised(slot):
        acc = acc_ref[slot]
        return (acc[:HEAD_DIM] * (1.0 / acc[HEAD_DIM:HEAD_DIM + 1])).T

    for s in range(n_seq):
        for pair in range(N_HEADS // 2):
            halves = [normalised(s * N_HEADS + 2 * pair), normalised(s * N_HEADS + 2 * pair + 1)]
            o_ref[s * tq:(s + 1) * tq, pair * V7X_LANES:(pair + 1) * V7X_LANES] = (
                jnp.concatenate(halves, axis=1).astype(o_ref.dtype))


def _attn_call(q, kd, vd, seq_len, mode, ctx=None, sink=None):
    ntok = q.shape[0]
    nb = ntok // seq_len
    tq = min(ATTN_Q_TILE, seq_len)
    nq = seq_len // tq
    n_seq = ATTN_SEQS_PER_STEP if nq == 1 else 1
    group = N_HEADS // N_KV
    in_specs = [
        pl.BlockSpec((n_seq * tq, D_Q), lambda b, i: (b * nq + i, 0)),
        pl.BlockSpec((n_seq * seq_len, 2 * D_KV), lambda b, i: (b, 0)),
        pl.BlockSpec((n_seq * seq_len, 2 * D_KV), lambda b, i: (b, 0)),
    ]
    args = [q, kd, vd]
    if ctx is not None:
        assert n_seq == 1
        past = ctx[0].shape[1]
        in_specs += [pl.BlockSpec((None, past, 2 * D_KV), lambda b, i: (b, 0, 0))] * 2
        args += list(ctx)
    if sink is not None:
        in_specs.append(pl.BlockSpec(memory_space=pltpu.SMEM))
        args.append(sink)
    return pl.pallas_call(
        functools.partial(_attn_kernel, tq=tq, seq_len=seq_len, n_seq=n_seq, mode=mode,
                          has_ctx=ctx is not None, has_sink=sink is not None),
        grid=(nb // n_seq, nq),
        in_specs=in_specs,
        out_specs=pl.BlockSpec((n_seq * tq, D_Q), lambda b, i: (b * nq + i, 0)),
        out_shape=jax.ShapeDtypeStruct((ntok, D_Q), BF16),
        scratch_shapes=[
            pltpu.VMEM((n_seq * N_HEADS, 1, tq), F32),
            pltpu.VMEM((n_seq * N_HEADS, HEAD_DIM + ATTN_ONES, tq), F32),
        ],
        compiler_params=_params("parallel", "parallel"),
        name="attn_" + mode + ("_sink" if sink is not None else ""),
    )(*args)


def _mix_out_kernel(oa_ref, ob_ref, oc_ref, x_ref, mod_ref, g_ref, wgate_ref,
                    woa_ref, wob_ref, woc_ref, wout_ref, o_ref):
    mod = mod_ref[0]
    x = x_ref[...]
    h = (_rms_rows(x) * (g_ref[...] * (1.0 + mod[1:2])) + mod[0:1]).astype(BF16)

    def branch(k, o_ref_k, w_ref_k):
        gate = _sigmoid(_dot(h, wgate_ref[:, k * D_MODEL:(k + 1) * D_MODEL]))
        return gate * _dot(o_ref_k[...], w_ref_k[...])

    merged = branch(0, oa_ref, woa_ref) + branch(1, ob_ref, wob_ref) + branch(2, oc_ref, woc_ref)
    out = _dot(merged.astype(BF16), wout_ref[...])
    o_ref[...] = x + mod[2:3] * out


def _mix_out_call(layer, oa, ob, oc, x, mod, g, w_gate, w_oa, w_ob, w_oc, w_out, seq_len):
    ntok = x.shape[0]
    tm = WIDE_TOKEN_TILE
    tiles_per_seq = max(seq_len // tm, 1)
    cond_of = (lambda i: i // tiles_per_seq) if mod.shape[0] > 1 else (lambda i: 0)
    row = lambda w: pl.BlockSpec((tm, w), lambda i: (i, 0))
    return pl.pallas_call(
        _mix_out_kernel,
        grid=(ntok // tm,),
        in_specs=[row(D_LRU), row(D_Q), row(D_Q), row(D_MODEL),
                  pl.BlockSpec((1, 6, D_MODEL), lambda i: (cond_of(i), 0, 0)),
                  _resident((1, D_MODEL)), _resident((D_MODEL, D_IN - D_TOKEN_MIX), lead=(layer,)),
                  _resident((D_LRU, D_MODEL), lead=(layer,)), _resident((D_Q, D_MODEL), lead=(layer,)),
                  _resident((D_Q, D_MODEL), lead=(layer,)), _resident((D_MODEL, D_MODEL), lead=(layer,))],
        out_specs=row(D_MODEL),
        out_shape=jax.ShapeDtypeStruct((ntok, D_MODEL), F32),
        compiler_params=_params("parallel"),
        name="mix_out",
    )(oa, ob, oc, x, mod, g, w_gate, w_oa, w_ob, w_oc, w_out)


def _ffn_kernel(*refs, tm, seq_len, final):
    span_tiles = seq_len > tm
    it = iter(refs)
    x_ref = next(it)
    prev_ref, next_ref = (next(it), next(it)) if span_tiles else (None, None)
    mod_ref, g_ref, wup_ref, cw_ref, cb_ref, wdn_ref = (next(it) for _ in range(6))
    fg_ref = next(it) if final else None
    o_ref = next(it)
    h_ref, act_ref = next(it), next(it)

    halo = V7X_SUBLANES_F32
    sub = min(tm, seq_len)
    mod = mod_ref[0]
    shift, scale, gate2 = mod[3:4], mod[4:5], mod[5:6]
    gain = g_ref[...] * (1.0 + scale)
    x = x_ref[...]

    def modulated(v):
        return (_rms_rows(v) * gain + shift).astype(BF16)

    if span_tiles:
        tile_in_seq = pl.program_id(0) % (seq_len // tm)
        at_start, at_end = tile_in_seq == 0, tile_in_seq == seq_len // tm - 1
        h_ref[0:halo, :] = modulated(prev_ref[...])
        h_ref[halo:halo + tm, :] = modulated(x)
        h_ref[halo + tm:, :] = modulated(next_ref[...])
        centre = slice(halo, halo + tm)
    else:
        h_ref[...] = modulated(x)
        centre = slice(None)

    cw = cw_ref[...]
    zero_halo = jnp.zeros((halo, FFN_CHUNK), F32)
    for c in range(D_FF // FFN_CHUNK):
        cols = slice(c * FFN_CHUNK, (c + 1) * FFN_CHUNK)
        vcols = slice(D_FF + c * FFN_CHUNK, D_FF + (c + 1) * FFN_CHUNK)
        pre = _dot(h_ref[...], wup_ref[:, cols])
        val = _dot(h_ref[centre, :], wup_ref[:, vcols])
        parts = []
        for b in range(tm // sub):
            if span_tiles:
                ext = jnp.concatenate([jnp.where(at_start, 0.0, pre[:halo]), pre[halo:halo + tm],
                                       jnp.where(at_end, 0.0, pre[halo + tm:])], axis=0)
            else:
                ext = jnp.concatenate([zero_halo, pre[b * sub:(b + 1) * sub], zero_halo], axis=0)
            before = pltpu.roll(ext, 1, 0)[halo:halo + sub]
            after = pltpu.roll(ext, sub + 2 * halo - 1, 0)[halo:halo + sub]
            parts.append(cb_ref[:, cols] + cw[0:1, cols] * before + cw[1:2, cols] * ext[halo:halo + sub]
                         + cw[2:3, cols] * after)
        gate = parts[0] if len(parts) == 1 else jnp.concatenate(parts, axis=0)
        act_ref[:, cols] = (_gelu_tanh(gate) * val).astype(BF16)

    y = x + gate2 * _dot(act_ref[...], wdn_ref[...])
    if final:
        y = _rms_rows(y) * fg_ref[...]
    o_ref[...] = y


def _ffn_call(layer, x, mod, g, w_up, conv_w, conv_b, w_down, seq_len, final_g=None):
    ntok = x.shape[0]
    tm = WIDE_TOKEN_TILE
    halo = V7X_SUBLANES_F32
    tiles_per_seq = max(seq_len // tm, 1)
    cond_of = (lambda i: i // tiles_per_seq) if mod.shape[0] > 1 else (lambda i: 0)
    n_halo_blocks = ntok // halo
    span_tiles = seq_len > tm
    in_specs = [pl.BlockSpec((tm, D_MODEL), lambda i: (i, 0))]
    args = [x]
    if span_tiles:
        in_specs += [
            pl.BlockSpec((halo, D_MODEL), lambda i: (jnp.maximum(i * (tm // halo) - 1, 0), 0)),
            pl.BlockSpec((halo, D_MODEL), lambda i: (jnp.minimum((i + 1) * (tm // halo), n_halo_blocks - 1), 0)),
        ]
        args += [x, x]
    in_specs += [
        pl.BlockSpec((1, 6, D_MODEL), lambda i: (cond_of(i), 0, 0)),
        _resident((1, D_MODEL)),
        _resident((D_MODEL, 2 * D_FF), lead=(layer,)),
        _resident((3, D_FF)),
        _resident((1, D_FF)),
        _resident((D_FF, D_MODEL), lead=(layer,)),
    ]
    args += [mod, g, w_up, conv_w, conv_b, w_down]
    h_rows = tm + 2 * halo if span_tiles else tm
    final = final_g is not None
    if final:
        in_specs.append(_resident((1, D_MODEL)))
        args.append(final_g)
    return pl.pallas_call(
        functools.partial(_ffn_kernel, tm=tm, seq_len=seq_len, final=final),
        grid=(ntok // tm,),
        in_specs=in_specs,
        out_specs=pl.BlockSpec((tm, D_MODEL), lambda i: (i, 0)),
        out_shape=jax.ShapeDtypeStruct((ntok, D_MODEL), F32),
        scratch_shapes=[
            pltpu.VMEM((h_rows, D_MODEL), BF16),
            pltpu.VMEM((tm, D_FF), BF16),
        ],
        compiler_params=_params("parallel"),
        name="ffn",
    )(*args)


def _rope_tables(seq_len):
    nf = HEAD_DIM // 4
    n_rows = seq_len // GRID_W
    inv = ROPE_THETA ** (-jnp.arange(nf, dtype=F32) / nf)
    ang_r = jnp.arange(n_rows, dtype=F32)[:, None] * inv[None, :]
    ang_c = jnp.arange(GRID_W, dtype=F32)[:, None] * inv[None, :]
    by_row = lambda a: jnp.repeat(a, GRID_W, axis=0)
    by_col = lambda a: jnp.tile(a, (n_rows, 1))
    cos_r, sin_r, cos_c, sin_c = by_row(jnp.cos(ang_r)), by_row(jnp.sin(ang_r)), by_col(jnp.cos(ang_c)), by_col(jnp.sin(ang_c))
    cos = jnp.concatenate([cos_r, cos_r, cos_c, cos_c], axis=1)
    sin = jnp.concatenate([-sin_r, sin_r, -sin_c, sin_c], axis=1)
    reps = V7X_LANES // HEAD_DIM
    return jnp.concatenate([cos] * reps, axis=1), jnp.concatenate([sin] * reps, axis=1)


def _segment_mean_matrix(width):
    seg = np.arange(width) // HEAD_DIM
    return jnp.asarray((seg[:, None] == seg[None, :]).astype(np.float32) / HEAD_DIM, dtype=BF16)


def _dup_cache(c):
    return jnp.concatenate([c[:, :, 0], c[:, :, 0], c[:, :, 1], c[:, :, 1]], axis=-1).astype(BF16)


def _ext_cache(c):
    b, n = c.shape[:2]
    ones = jnp.ones((b, n, ATTN_ONES), c.dtype)
    zeros = jnp.zeros((b, n, V7X_LANES - HEAD_DIM - ATTN_ONES), c.dtype)
    return jnp.concatenate([c[:, :, 0], ones, zeros, c[:, :, 1], ones, zeros], axis=-1).astype(BF16)


def kernel(x_prompt, x_sample, c, cache_kb, cache_vb, cache_kc, cache_vc, state_lru, c_ctx, norm1_g, norm2_g, w_mod, b_mod, w_in, lru_conv_w, lru_conv_b, lru_wa, lru_ba, lru_wx, lru_bx, lru_lam, qnorm_g, knorm_g, sink_c, w_oa, w_ob, w_oc, w_out, w_up, ffn_conv_w, ffn_conv_b, w_down, final_g):
    batch, seq, d = x_prompt.shape
    dec_batch, dec_seq, _ = x_sample.shape
    depth = w_in.shape[0]
    assert d == D_MODEL and depth == DEPTH and dec_batch + 1 <= 8

    cond8 = jnp.zeros((8, d), F32).at[0].set(c_ctx).at[1:1 + dec_batch].set(c)
    mod_all = _mod_call(cond8, w_mod, b_mod).reshape(depth, 8, 6, d)

    w_in_b = w_in[:, :, :D_TOKEN_MIX].astype(BF16)
    w_bgate_b = w_in[:, :, D_TOKEN_MIX:].astype(BF16)
    w_oa_b, w_ob_b, w_oc_b = (w.astype(BF16) for w in (w_oa, w_ob, w_oc))
    w_out_b, w_up_b, w_down_b = (w.astype(BF16) for w in (w_out, w_up, w_down))
    w_gate = (0.5 * jnp.concatenate([lru_wa, lru_wx], axis=-1)).astype(BF16)
    half_ba, half_bx = 0.5 * lru_ba, 0.5 * lru_bx
    seg_q, seg_k = _segment_mean_matrix(D_Q), _segment_mean_matrix(D_KV)
    rope_tabs = _rope_tables(dec_seq)
    qg = jnp.tile(qnorm_g, (1, N_HEADS))
    kg = jnp.tile(knorm_g, (1, N_KV))

    def layer(l, x, mod, seq_len, h0, ctx, is_last):
        is_ctx = ctx is None
        ntok = x.shape[0]
        outs = _in_proj_call(l, x, mod, norm1_g[l][None], w_in_b, qg[l][None], kg[l][None], seg_q, seg_k,
                             lru_conv_w[l], lru_conv_b[l][None], None if is_ctx else rope_tabs, seq_len,
                             emit_kv=is_ctx)
        xc, gya, qb, kbd, vbd, qc, kcd, vcd = outs[:8]
        per_seq = lambda a: a.reshape(ntok // seq_len, seq_len, D_LRU)
        lru_args = lambda dr: (w_gate, half_ba[l, dr][None], half_bx[l, dr][None], lru_lam[l, dr][None],
                               h0[:, dr].reshape(-1, D_LRU // V7X_LANES, V7X_LANES), seq_len)
        hf, last_f = _lru_call(l, per_seq(xc), *lru_args(0), reverse=False)
        oa, last_b = _lru_call(l, per_seq(xc), *lru_args(1), reverse=True, hf=hf, gya=per_seq(gya))
        oa = oa.reshape(ntok, D_LRU)
        if is_ctx:
            ob = _attn_call(qb, kbd, vbd, seq_len, "full")
            oc = _attn_call(qc, kcd, vcd, seq_len, "full", sink=sink_c[l])
        else:
            ob = _attn_call(qb, kbd, vbd, seq_len, "full", ctx=(ctx["kb"], ctx["vb"]))
            oc = _attn_call(qc, kcd, vcd, seq_len, "window", ctx=(ctx["kc"], ctx["vc"]), sink=sink_c[l])
        x = _mix_out_call(l, oa, ob, oc, x, mod, norm1_g[l][None], w_bgate_b, w_oa_b, w_ob_b, w_oc_b, w_out_b,
                          seq_len)
        x = _ffn_call(l, x, mod, norm2_g[l][None], w_up_b, ffn_conv_w[l], ffn_conv_b[l][None], w_down_b,
                      seq_len, final_g=final_g[None] if is_last else None)
        return x, outs[8:], (last_f, last_b)

    xp = x_prompt.reshape(batch * seq, d)
    xs = x_sample.reshape(dec_batch * dec_seq, d)
    zeros_h0 = jnp.zeros((batch, 2, D_LRU), F32)
    kbs, vbs, kcs, vcs, lrus = [], [], [], [], []
    for l in range(depth):
        is_last = l == depth - 1
        xp, (kb, vb, kc, vc), (last_f, last_b) = layer(l, xp, mod_all[l, 0:1], seq, zeros_h0, None, is_last)
        kbs.append(kb)
        vbs.append(vb)
        kcs.append(kc)
        vcs.append(vc)
        lrus.append(jnp.stack([last_f.reshape(batch, D_LRU), last_b.reshape(batch, D_LRU)], axis=1))
        cached = {"kb": _dup_cache(cache_kb[:, l]), "vb": _ext_cache(cache_vb[:, l]),
                  "kc": _dup_cache(cache_kc[:, l]), "vc": _ext_cache(cache_vc[:, l])}
        xs, _, _ = layer(l, xs, mod_all[l, 1:1 + dec_batch], dec_seq, state_lru[:, l], cached, is_last)

    def stack_kv(parts):
        return jnp.stack([p.reshape(batch, seq, N_KV, HEAD_DIM) for p in parts], axis=1)

    y_prompt = xp.reshape(batch, seq, d)
    y_sample = xs.reshape(dec_batch, dec_seq, d)
    return (y_prompt, y_sample, stack_kv(kbs), stack_kv(vbs), stack_kv(kcs), stack_kv(vcs),
            jnp.stack(lrus, axis=1))
```

```python
import functools

import jax
import jax.numpy as jnp
import numpy as np
from jax import lax
from jax.experimental import pallas as pl
from jax.experimental.pallas import tpu as pltpu

D_MODEL = 1024
DEPTH = 2
GRID_W = 64
D_LRU = 1024
LRU_BLOCKS = 8
LRU_BLOCK = D_LRU // LRU_BLOCKS
LRU_C = 8.0
HEAD_DIM = 64
N_HEADS = 8
N_KV = 2
D_Q = N_HEADS * HEAD_DIM
D_KV = N_KV * HEAD_DIM
WINDOW = 128
D_FF = 2816
ROPE_THETA = 10000.0
NORM_EPS = 1e-6
NEG_INF = -1e30
ATTN_SCALE = HEAD_DIM ** -0.5
LOG2_E = 1.4426950408889634
LN_2 = 0.6931471805599453
IN_SIZES = (D_LRU, D_LRU, D_Q, D_KV, D_KV, D_Q, D_KV, D_KV, D_MODEL, D_MODEL, D_MODEL)
IN_OFF = tuple(int(v) for v in np.cumsum((0,) + IN_SIZES))
D_IN = IN_OFF[-1]
D_TOKEN_MIX = IN_OFF[8]

V7X_LANES = 128
V7X_SUBLANES_F32 = 8
V7X_VMEM_LIMIT_BYTES = 56 * 1024 * 1024

TOKEN_TILE = 512
WIDE_TOKEN_TILE = 1024
FFN_CHUNK = 256
ATTN_Q_TILE = 256
ATTN_KEY_TILE = 256
ATTN_SEQS_PER_STEP = 8
ATTN_TILES_PER_STEP = 4
ATTN_LOOKAHEAD = 5
ATTN_ONES = 16
LRU_CHUNK = 512
LRU_SEQS_PER_STEP = 4
LRU_SCAN_UNROLL = 8
MOD_COL_TILE = 1536

F32 = jnp.float32
BF16 = jnp.bfloat16


def _params(*sem):
    return pltpu.CompilerParams(dimension_semantics=sem, vmem_limit_bytes=V7X_VMEM_LIMIT_BYTES)


def _resident(shape, lead=()):
    nd = len(shape)
    lead = tuple(int(i) for i in lead)
    return pl.BlockSpec((None,) * len(lead) + tuple(shape), lambda *_: lead + (0,) * nd,
                        pipeline_mode=pl.Buffered(1))


def _sigmoid(x):
    return 1.0 / (1.0 + jnp.exp(-x))


GELU_C0 = 0.7978845608028654
GELU_C1 = 0.044715 * GELU_C0


def _gelu_tanh(x):
    half = 0.5 * x
    return half + half * jnp.tanh(x * (GELU_C0 + GELU_C1 * (x * x)))


def _rms_rows(x):
    return x * lax.rsqrt(jnp.mean(x * x, axis=-1, keepdims=True) + NORM_EPS)


def _dot(a, b):
    return jnp.dot(a, b, preferred_element_type=F32)


def _mod_kernel(c_ref, w_ref, b_ref, o_ref):
    c = c_ref[...]
    s = (c * _sigmoid(c)).astype(BF16)
    o_ref[0] = _dot(s, w_ref[0].astype(BF16)) + b_ref[0]


def _mod_call(cond8, w_mod, b_mod):
    depth, d, n = w_mod.shape
    tn = MOD_COL_TILE
    return pl.pallas_call(
        _mod_kernel,
        grid=(depth, n // tn),
        in_specs=[
            pl.BlockSpec((8, d), lambda l, j: (0, 0)),
            pl.BlockSpec((1, d, tn), lambda l, j: (l, 0, j)),
            pl.BlockSpec((1, 1, tn), lambda l, j: (l, 0, j)),
        ],
        out_specs=pl.BlockSpec((1, 8, tn), lambda l, j: (l, 0, j)),
        out_shape=jax.ShapeDtypeStruct((depth, 8, n), F32),
        compiler_params=_params("parallel", "parallel"),
        name="mod",
    )(cond8, w_mod, b_mod.reshape(depth, 1, n))


def _head_rms(x, seg_ref, g):
    sq = x * x
    hi = sq.astype(BF16)
    lo = (sq - hi.astype(F32)).astype(BF16)
    ms = _dot(hi, seg_ref[...]) + _dot(lo, seg_ref[...])
    return x * lax.rsqrt(ms + NORM_EPS) * g


def _rope(x, cos, sin):
    w = x.shape[1]
    lane = lax.broadcasted_iota(jnp.int32, x.shape, 1)
    fwd = pltpu.roll(x, w - 16, 1)
    bwd = pltpu.roll(x, 16, 1)
    partner = jnp.where((lane % 32) < 16, fwd, bwd)
    return x * cos + partner * sin


def _dup_heads(x):
    lane = lax.broadcasted_iota(jnp.int32, x.shape, 1)
    swapped = pltpu.roll(x, HEAD_DIM, 1)
    low = lane < HEAD_DIM
    return jnp.concatenate([jnp.where(low, x, swapped), jnp.where(low, swapped, x)], axis=1)


def _ext_values(x):
    lane = lax.broadcasted_iota(jnp.int32, x.shape, 1)
    swapped = pltpu.roll(x, HEAD_DIM, 1)
    low = lane < HEAD_DIM
    tail = jnp.where(lane < HEAD_DIM + ATTN_ONES, 1.0, 0.0)
    return jnp.concatenate([jnp.where(low, x, tail), jnp.where(low, swapped, tail)], axis=1)


def _lru_conv(ext, cw, cb, n):
    halo = V7X_SUBLANES_F32
    taps = (pltpu.roll(ext, 2, 0), pltpu.roll(ext, 1, 0), ext, pltpu.roll(ext, n + 2 * halo - 1, 0))
    out = cb + cw[0:1] * taps[0][halo:halo + n]
    for k in range(1, 4):
        out = out + cw[k:k + 1] * taps[k][halo:halo + n]
    return out


def _in_proj_kernel(*refs, tm, seq_len, rope, emit_kv):
    span_tiles = seq_len > tm
    it = iter(refs)
    x_ref = next(it)
    prev_ref, next_ref = (next(it), next(it)) if span_tiles else (None, None)
    mod_ref, g_ref, w_ref, qg_ref, kg_ref, seg_q_ref, seg_k_ref, cw_ref, cb_ref = (next(it) for _ in range(9))
    cos_ref = sin_ref = None
    if rope:
        cos_ref, sin_ref = next(it), next(it)
    xc_ref, gya_ref, qb_ref, kbd_ref, vbd_ref, qc_ref, kcd_ref, vcd_ref = (next(it) for _ in range(8))
    if emit_kv:
        kb_ref, vb_ref, kc_ref, vc_ref = (next(it) for _ in range(4))

    mod = mod_ref[0]
    shift, scale = mod[0:1], mod[1:2]
    gain = g_ref[...] * (1.0 + scale)

    def modulated(v):
        return (_rms_rows(v) * gain + shift).astype(BF16)

    h = modulated(x_ref[...])

    def seg(lo, hi):
        return _dot(h, w_ref[:, IN_OFF[lo]:IN_OFF[hi]])

    halo = V7X_SUBLANES_F32
    qb_raw = seg(2, 3)
    kv_b = seg(3, 5)
    qkv_c = seg(5, 8)
    xa = seg(0, 1)
    if span_tiles:
        h_halo = modulated(jnp.concatenate([prev_ref[...], next_ref[...]], axis=0))
        xa_halo = _dot(h_halo, w_ref[:, IN_OFF[0]:IN_OFF[1]])
    qb = _head_rms(qb_raw, seg_q_ref, qg_ref[...])
    kb = _head_rms(kv_b[:, :D_KV], seg_k_ref, kg_ref[...])
    vb = kv_b[:, D_KV:]
    qc = qkv_c[:, :D_Q]
    kc = qkv_c[:, D_Q:D_Q + D_KV]
    vc = qkv_c[:, D_Q + D_KV:]
    gya_ref[...] = _gelu_tanh(seg(1, 2)).astype(BF16)

    cw, cb = cw_ref[...], cb_ref[...]
    if span_tiles:
        tile_in_seq = pl.program_id(0) % (seq_len // tm)
        ext = jnp.concatenate([jnp.where(tile_in_seq == 0, 0.0, xa_halo[:halo]), xa,
                               jnp.where(tile_in_seq == seq_len // tm - 1, 0.0, xa_halo[halo:])], axis=0)
        xc_ref[...] = _lru_conv(ext, cw, cb, tm)
    else:
        zero_halo = jnp.zeros((halo, D_LRU), F32)
        for b in range(tm // seq_len):
            rows = slice(b * seq_len, (b + 1) * seq_len)
            ext = jnp.concatenate([zero_halo, xa[rows], zero_halo], axis=0)
            xc_ref[rows, :] = _lru_conv(ext, cw, cb, seq_len)
    if emit_kv:
        kb_ref[...] = kb
        vb_ref[...] = vb
        kc_ref[...] = kc
        vc_ref[...] = vc
    if rope:
        cos, sin = cos_ref[...], sin_ref[...]
        cos_q = jnp.concatenate([cos] * (D_Q // V7X_LANES), axis=1)
        sin_q = jnp.concatenate([sin] * (D_Q // V7X_LANES), axis=1)
        qb = _rope(qb, cos_q, sin_q)
        qc = _rope(qc, cos_q, sin_q)
        kb = _rope(kb, cos, sin)
        kc = _rope(kc, cos, sin)
    qb_ref[...] = (qb * (ATTN_SCALE * LOG2_E)).astype(BF16)
    qc_ref[...] = (qc * (ATTN_SCALE * LOG2_E)).astype(BF16)
    kbd_ref[...] = _dup_heads(kb).astype(BF16)
    vbd_ref[...] = _ext_values(vb).astype(BF16)
    kcd_ref[...] = _dup_heads(kc).astype(BF16)
    vcd_ref[...] = _ext_values(vc).astype(BF16)


def _in_proj_call(layer, x, mod, g, w_in, qg, kg, seg_q, seg_k, conv_w, conv_b, rope_tabs, seq_len, emit_kv):
    ntok = x.shape[0]
    tm = TOKEN_TILE
    halo = V7X_SUBLANES_F32
    tiles_per_seq = max(seq_len // tm, 1)
    ncond = mod.shape[0]
    cond_of = (lambda i: i // tiles_per_seq) if ncond > 1 else (lambda i: 0)
    row = lambda w: pl.BlockSpec((tm, w), lambda i: (i, 0))
    in_specs = [row(D_MODEL)]
    args = [x]
    if seq_len > tm:
        n_halo_blocks = ntok // halo
        in_specs += [
            pl.BlockSpec((halo, D_MODEL), lambda i: (jnp.maximum(i * (tm // halo) - 1, 0), 0)),
            pl.BlockSpec((halo, D_MODEL), lambda i: (jnp.minimum((i + 1) * (tm // halo), n_halo_blocks - 1), 0)),
        ]
        args += [x, x]
    in_specs += [
        pl.BlockSpec((1, 6, D_MODEL), lambda i: (cond_of(i), 0, 0)),
        _resident((1, D_MODEL)),
        _resident((D_MODEL, D_TOKEN_MIX), lead=(layer,)),
        _resident((1, D_Q)),
        _resident((1, D_KV)),
        _resident((D_Q, D_Q)),
        _resident((D_KV, D_KV)),
        _resident((4, D_LRU)),
        _resident((1, D_LRU)),
    ]
    args += [mod, g, w_in, qg, kg, seg_q, seg_k, conv_w, conv_b]
    rope = rope_tabs is not None
    if rope:
        in_specs += [pl.BlockSpec((tm, V7X_LANES), lambda i: (i % tiles_per_seq, 0))] * 2
        args += list(rope_tabs)
    widths = [D_LRU, D_LRU, D_Q, 2 * D_KV, 2 * D_KV, D_Q, 2 * D_KV, 2 * D_KV]
    out_specs = [row(w) for w in widths]
    out_shape = [jax.ShapeDtypeStruct((ntok, w), F32 if n == 0 else BF16) for n, w in enumerate(widths)]
    if emit_kv:
        out_specs += [row(D_KV)] * 4
        out_shape += [jax.ShapeDtypeStruct((ntok, D_KV), F32)] * 4
    return pl.pallas_call(
        functools.partial(_in_proj_kernel, tm=tm, seq_len=seq_len, rope=rope, emit_kv=emit_kv),
        grid=(ntok // tm,),
        in_specs=in_specs,
        out_specs=out_specs,
        out_shape=out_shape,
        compiler_params=_params("parallel"),
        name="in_proj",
    )(*args)


def _lru_kernel(*refs, chunk, n_chunks, n_seq, reverse):
    it = iter(refs)
    xc_ref, wg_ref, ba_ref, bx_ref, lam_ref, h0_ref = (next(it) for _ in range(6))
    if reverse:
        hf_ref, gya_ref = next(it), next(it)
    out_ref, last_ref = next(it), next(it)
    a_ref, u_ref, hs_ref, state_ref = next(it), next(it), next(it), next(it)

    fold = D_LRU // V7X_LANES

    def folded_block(c):
        return pl.ds(c, chunk, stride=fold)

    j = pl.program_id(1)
    lam = lam_ref[...]
    softplus_neg_lam = jnp.maximum(-lam, 0.0) + jnp.log1p(jnp.exp(-jnp.abs(lam)))
    slope = (-0.5 * LRU_C * LOG2_E) * softplus_neg_lam

    for s in range(n_seq):
        xc = xc_ref[s]
        xc_b = xc.astype(BF16)
        half_xc = 0.5 * xc

        for n in range(LRU_BLOCKS):
            cols = slice(n * LRU_BLOCK, (n + 1) * LRU_BLOCK)
            z = _dot(xc_b[:, cols], wg_ref[n])
            t_r = jnp.tanh(z[:, :LRU_BLOCK] + ba_ref[:, cols])
            t_i = jnp.tanh(z[:, LRU_BLOCK:] + bx_ref[:, cols])
            log2_a = slope[:, cols] + slope[:, cols] * t_r
            a = jnp.exp2(log2_a)
            one_minus_a2 = jnp.tanh(log2_a * (-LN_2)) * (a * a + 1.0)
            root = jnp.where(one_minus_a2 > 0.0, one_minus_a2 * lax.rsqrt(one_minus_a2), 0.0)
            hx = half_xc[:, cols]
            a_ref[s, folded_block(n), :] = a
            u_ref[s, folded_block(n), :] = root * (hx + hx * t_i)

    @pl.when(j == 0)
    def _():
        state_ref[...] = h0_ref[...]

    steps_per_trip = LRU_SCAN_UNROLL

    def trip(gi, hs):
        gidx = (chunk // steps_per_trip - 1 - gi) if reverse else gi
        base = pl.multiple_of(gidx * (steps_per_trip * fold), steps_per_trip * fold)
        hs = list(hs)
        for k in range(steps_per_trip):
            rows = pl.ds(base + (steps_per_trip - 1 - k if reverse else k) * fold, fold)
            for s in range(n_seq):
                hs[s] = a_ref[s, rows, :] * hs[s] + u_ref[s, rows, :]
                hs_ref[s, rows, :] = hs[s]
        return tuple(hs)

    hs = lax.fori_loop(0, chunk // steps_per_trip, trip, tuple(state_ref[s] for s in range(n_seq)))
    for s in range(n_seq):
        state_ref[s] = hs[s]

    for s in range(n_seq):
        for c in range(fold):
            cols = slice(c * V7X_LANES, (c + 1) * V7X_LANES)
            scanned = hs_ref[s, folded_block(c), :]
            if reverse:
                out_ref[s, :, cols] = ((hf_ref[s, :, cols] + scanned)
                                       * gya_ref[s, :, cols].astype(F32)).astype(out_ref.dtype)
            else:
                out_ref[s, :, cols] = scanned

    @pl.when(j == n_chunks - 1)
    def _():
        for s in range(n_seq):
            last_ref[s] = hs[s]


def _lru_call(layer, xc, w_gate, ba, bx, lam, h0, seq_len, reverse, hf=None, gya=None):
    nb = xc.shape[0]
    chunk = min(LRU_CHUNK, seq_len)
    nt = seq_len // chunk
    n_seq = min(LRU_SEQS_PER_STEP, nb)
    fold = D_LRU // V7X_LANES
    pos = (lambda j: nt - 1 - j) if reverse else (lambda j: j)
    cur = lambda b, j: (b, pos(j), 0)
    per_seq = pl.BlockSpec((n_seq, fold, V7X_LANES), lambda b, j: (b, 0, 0))
    in_specs = [
        pl.BlockSpec((n_seq, chunk, D_LRU), cur),
        _resident((LRU_BLOCKS, LRU_BLOCK, 2 * LRU_BLOCK), lead=(layer, int(reverse))),
        _resident((1, D_LRU)),
        _resident((1, D_LRU)),
        _resident((1, D_LRU)),
        per_seq,
    ]
    args = [xc, w_gate, ba, bx, lam, h0]
    folded = pltpu.VMEM((n_seq, chunk * fold, V7X_LANES), F32)
    scratch = [folded, folded, folded, pltpu.VMEM((n_seq, fold, V7X_LANES), F32)]
    if reverse:
        in_specs += [pl.BlockSpec((n_seq, chunk, D_LRU), cur)] * 2
        args += [hf, gya]
    return pl.pallas_call(
        functools.partial(_lru_kernel, chunk=chunk, n_chunks=nt, n_seq=n_seq, reverse=reverse),
        grid=(nb // n_seq, nt),
        in_specs=in_specs,
        out_specs=[pl.BlockSpec((n_seq, chunk, D_LRU), cur), per_seq],
        out_shape=[jax.ShapeDtypeStruct((nb, seq_len, D_LRU), BF16 if reverse else F32),
                   jax.ShapeDtypeStruct((nb, fold, V7X_LANES), F32)],
        scratch_shapes=scratch,
        compiler_params=_params("arbitrary", "arbitrary"),
        name="lru_bwd" if reverse else "lru_fwd",
    )(*args)


def _attn_kernel(*refs, tq, seq_len, n_seq, mode, has_ctx, has_sink):
    it = iter(refs)
    q_ref, k_ref, v_ref = next(it), next(it), next(it)
    ck_ref = cv_ref = sink_ref = None
    if has_ctx:
        ck_ref, cv_ref = next(it), next(it)
    if has_sink:
        sink_ref = next(it)
    o_ref = next(it)
    m_ref, acc_ref = next(it), next(it)

    qi = pl.program_id(1)
    group = N_HEADS // N_KV
    acc_rows = HEAD_DIM + ATTN_ONES
    acc_row = lax.broadcasted_iota(jnp.int32, (acc_rows, tq), 0)
    lane = lax.broadcasted_iota(jnp.int32, (tq, V7X_LANES), 1)
    low = lane < HEAD_DIM
    key_tile = ATTN_KEY_TILE

    slots = [(s, h) for s in range(n_seq) for h in range(N_HEADS)]
    qh = []
    for slot, (s, h) in enumerate(slots):
        qp = q_ref[s * tq:(s + 1) * tq, (h // 2) * V7X_LANES:(h // 2 + 1) * V7X_LANES]
        zero = jnp.zeros_like(qp)
        qh.append(jnp.where(low, zero, qp) if h % 2 else jnp.where(low, qp, zero))
        if has_sink:
            m_ref[slot] = jnp.full((1, tq), sink_ref[h] * LOG2_E, F32)
            acc_ref[slot] = jnp.where(acc_row < HEAD_DIM, 0.0, 1.0)
        else:
            m_ref[slot] = jnp.full((1, tq), NEG_INF, F32)
            acc_ref[slot] = jnp.zeros((acc_rows, tq), F32)

    def scores(item):
        slot, key_ref, _, keys, _ = item
        g = slots[slot][1] // group
        k = key_ref[keys, g * V7X_LANES:(g + 1) * V7X_LANES]
        return lax.dot_general(k, qh[slot], (((1,), (1,)), ((), ())), preferred_element_type=F32)

    def absorb(item, st):
        slot, _, val_ref, keys, mask = item
        g = slots[slot][1] // group
        vt = val_ref[keys, g * V7X_LANES:(g + 1) * V7X_LANES].T[:acc_rows]
        if mask is not None:
            st = jnp.where(mask, st, NEG_INF)
        m_old = m_ref[slot]
        m_new = jnp.maximum(m_old, jnp.max(st, axis=0, keepdims=True))
        alpha = jnp.exp2(m_old - m_new)
        p = jnp.exp2(st - m_new)
        acc_ref[slot] = alpha * acc_ref[slot] + _dot(vt, p.astype(BF16))
        m_ref[slot] = m_new

    def run(items):
        pending = {}
        for i in range(len(items) + ATTN_LOOKAHEAD):
            if i < len(items):
                pending[i] = scores(items[i])
            j = i - ATTN_LOOKAHEAD
            if j >= 0:
                absorb(items[j], pending.pop(j))

    def tile_items(key_ref, val_ref, keys_of_seq, mask):
        return [(slot, key_ref, val_ref, keys_of_seq(s), mask) for slot, (s, _) in enumerate(slots)]

    head_items = tile_items(ck_ref, cv_ref, lambda s: slice(None), None) if has_ctx else []
    if mode == "full":
        n_tiles = seq_len // key_tile
        if n_tiles == 1:
            run(head_items + tile_items(k_ref, v_ref, lambda s: slice(s * seq_len, (s + 1) * seq_len), None))
        else:
            per_step = ATTN_TILES_PER_STEP

            def trip_items(t):
                items = []
                for u in range(per_step):
                    keys = pl.ds(pl.multiple_of((t * per_step + u) * key_tile, key_tile), key_tile)
                    items += tile_items(k_ref, v_ref, lambda s: keys, None)
                return items

            run(head_items + trip_items(0))

            def body(t, carry):
                run(trip_items(t))
                return carry
            lax.fori_loop(1, n_tiles // per_step, body, 0)
    else:
        span = tq + 2 * WINDOW
        q0 = qi * tq
        start = pl.multiple_of(jnp.clip(q0 - WINDOW, 0, seq_len - span), WINDOW)
        items = head_items
        for u in range(span // key_tile):
            k_abs = start + u * key_tile + lax.broadcasted_iota(jnp.int32, (key_tile, tq), 0)
            q_abs = q0 + lax.broadcasted_iota(jnp.int32, (key_tile, tq), 1)
            keys = pl.ds(start + u * key_tile, key_tile)
            items = items + tile_items(k_ref, v_ref, lambda s: keys, jnp.abs(q_abs - k_abs) <= WINDOW)
        run(items)

    def normalised(slot):
        acc = acc_ref[slot]
        return (acc[:HEAD_DIM] * (1.0 / acc[HEAD_DIM:HEAD_DIM + 1])).T

    for s in range(n_seq):
        for pair in range(N_HEADS // 2):
            halves = [normalised(s * N_HEADS + 2 * pair), normalised(s * N_HEADS + 2 * pair + 1)]
            o_ref[s * tq:(s + 1) * tq, pair * V7X_LANES:(pair + 1) * V7X_LANES] = (
                jnp.concatenate(halves, axis=1).astype(o_ref.dtype))


def _attn_call(q, kd, vd, seq_len, mode, ctx=None, sink=None):
    ntok = q.shape[0]
    nb = ntok // seq_len
    tq = min(ATTN_Q_TILE, seq_len)
    nq = seq_len // tq
    n_seq = ATTN_SEQS_PER_STEP if nq == 1 else 1
    group = N_HEADS // N_KV
    in_specs = [
        pl.BlockSpec((n_seq * tq, D_Q), lambda b, i: (b * nq + i, 0)),
        pl.BlockSpec((n_seq * seq_len, 2 * D_KV), lambda b, i: (b, 0)),
        pl.BlockSpec((n_seq * seq_len, 2 * D_KV), lambda b, i: (b, 0)),
    ]
    args = [q, kd, vd]
    if ctx is not None:
        assert n_seq == 1
        past = ctx[0].shape[1]
        in_specs += [pl.BlockSpec((None, past, 2 * D_KV), lambda b, i: (b, 0, 0))] * 2
        args += list(ctx)
    if sink is not None:
        in_specs.append(pl.BlockSpec(memory_space=pltpu.SMEM))
        args.append(sink)
    return pl.pallas_call(
        functools.partial(_attn_kernel, tq=tq, seq_len=seq_len, n_seq=n_seq, mode=mode,
                          has_ctx=ctx is not None, has_sink=sink is not None),
        grid=(nb // n_seq, nq),
        in_specs=in_specs,
        out_specs=pl.BlockSpec((n_seq * tq, D_Q), lambda b, i: (b * nq + i, 0)),
        out_shape=jax.ShapeDtypeStruct((ntok, D_Q), BF16),
        scratch_shapes=[
            pltpu.VMEM((n_seq * N_HEADS, 1, tq), F32),
            pltpu.VMEM((n_seq * N_HEADS, HEAD_DIM + ATTN_ONES, tq), F32),
        ],
        compiler_params=_params("parallel", "parallel"),
        name="attn_" + mode + ("_sink" if sink is not None else ""),
    )(*args)


def _mix_out_kernel(oa_ref, ob_ref, oc_ref, x_ref, mod_ref, g_ref, wgate_ref,
                    woa_ref, wob_ref, woc_ref, wout_ref, o_ref):
    mod = mod_ref[0]
    x = x_ref[...]
    h = (_rms_rows(x) * (g_ref[...] * (1.0 + mod[1:2])) + mod[0:1]).astype(BF16)

    def branch(k, o_ref_k, w_ref_k):
        gate = _sigmoid(_dot(h, wgate_ref[:, k * D_MODEL:(k + 1) * D_MODEL]))
        return gate * _dot(o_ref_k[...], w_ref_k[...])

    merged = branch(0, oa_ref, woa_ref) + branch(1, ob_ref, wob_ref) + branch(2, oc_ref, woc_ref)
    out = _dot(merged.astype(BF16), wout_ref[...])
    o_ref[...] = x + mod[2:3] * out


def _mix_out_call(layer, oa, ob, oc, x, mod, g, w_gate, w_oa, w_ob, w_oc, w_out, seq_len):
    ntok = x.shape[0]
    tm = WIDE_TOKEN_TILE
    tiles_per_seq = max(seq_len // tm, 1)
    cond_of = (lambda i: i // tiles_per_seq) if mod.shape[0] > 1 else (lambda i: 0)
    row = lambda w: pl.BlockSpec((tm, w), lambda i: (i, 0))
    return pl.pallas_call(
        _mix_out_kernel,
        grid=(ntok // tm,),
        in_specs=[row(D_LRU), row(D_Q), row(D_Q), row(D_MODEL),
                  pl.BlockSpec((1, 6, D_MODEL), lambda i: (cond_of(i), 0, 0)),
                  _resident((1, D_MODEL)), _resident((D_MODEL, D_IN - D_TOKEN_MIX), lead=(layer,)),
                  _resident((D_LRU, D_MODEL), lead=(layer,)), _resident((D_Q, D_MODEL), lead=(layer,)),
                  _resident((D_Q, D_MODEL), lead=(layer,)), _resident((D_MODEL, D_MODEL), lead=(layer,))],
        out_specs=row(D_MODEL),
        out_shape=jax.ShapeDtypeStruct((ntok, D_MODEL), F32),
        compiler_params=_params("parallel"),
        name="mix_out",
    )(oa, ob, oc, x, mod, g, w_gate, w_oa, w_ob, w_oc, w_out)


def _ffn_kernel(*refs, tm, seq_len, final):
    span_tiles = seq_len > tm
    it = iter(refs)
    x_ref = next(it)
    prev_ref, next_ref = (next(it), next(it)) if span_tiles else (None, None)
    mod_ref, g_ref, wup_ref, cw_ref, cb_ref, wdn_ref = (next(it) for _ in range(6))
    fg_ref = next(it) if final else None
    o_ref = next(it)
    h_ref, act_ref = next(it), next(it)

    halo = V7X_SUBLANES_F32
    sub = min(tm, seq_len)
    mod = mod_ref[0]
    shift, scale, gate2 = mod[3:4], mod[4:5], mod[5:6]
    gain = g_ref[...] * (1.0 + scale)
    x = x_ref[...]

    def modulated(v):
        return (_rms_rows(v) * gain + shift).astype(BF16)

    if span_tiles:
        tile_in_seq = pl.program_id(0) % (seq_len // tm)
        at_start, at_end = tile_in_seq == 0, tile_in_seq == seq_len // tm - 1
        h_ref[0:halo, :] = modulated(prev_ref[...])
        h_ref[halo:halo + tm, :] = modulated(x)
        h_ref[halo + tm:, :] = modulated(next_ref[...])
        centre = slice(halo, halo + tm)
    else:
        h_ref[...] = modulated(x)
        centre = slice(None)

    cw = cw_ref[...]
    zero_halo = jnp.zeros((halo, FFN_CHUNK), F32)
    for c in range(D_FF // FFN_CHUNK):
        cols = slice(c * FFN_CHUNK, (c + 1) * FFN_CHUNK)
        vcols = slice(D_FF + c * FFN_CHUNK, D_FF + (c + 1) * FFN_CHUNK)
        pre = _dot(h_ref[...], wup_ref[:, cols])
        val = _dot(h_ref[centre, :], wup_ref[:, vcols])
        parts = []
        for b in range(tm // sub):
            if span_tiles:
                ext = jnp.concatenate([jnp.where(at_start, 0.0, pre[:halo]), pre[halo:halo + tm],
                                       jnp.where(at_end, 0.0, pre[halo + tm:])], axis=0)
            else:
                ext = jnp.concatenate([zero_halo, pre[b * sub:(b + 1) * sub], zero_halo], axis=0)
            before = pltpu.roll(ext, 1, 0)[halo:halo + sub]
            after = pltpu.roll(ext, sub + 2 * halo - 1, 0)[halo:halo + sub]
            parts.append(cb_ref[:, cols] + cw[0:1, cols] * before + cw[1:2, cols] * ext[halo:halo + sub]
                         + cw[2:3, cols] * after)
        gate = parts[0] if len(parts) == 1 else jnp.concatenate(parts, axis=0)
        act_ref[:, cols] = (_gelu_tanh(gate) * val).astype(BF16)

    y = x + gate2 * _dot(act_ref[...], wdn_ref[...])
    if final:
        y = _rms_rows(y) * fg_ref[...]
    o_ref[...] = y


def _ffn_call(layer, x, mod, g, w_up, conv_w, conv_b, w_down, seq_len, final_g=None):
    ntok = x.shape[0]
    tm = WIDE_TOKEN_TILE
    halo = V7X_SUBLANES_F32
    tiles_per_seq = max(seq_len // tm, 1)
    cond_of = (lambda i: i // tiles_per_seq) if mod.shape[0] > 1 else (lambda i: 0)
    n_halo_blocks = ntok // halo
    span_tiles = seq_len > tm
    in_specs = [pl.BlockSpec((tm, D_MODEL), lambda i: (i, 0))]
    args = [x]
    if span_tiles:
        in_specs += [
            pl.BlockSpec((halo, D_MODEL), lambda i: (jnp.maximum(i * (tm // halo) - 1, 0), 0)),
            pl.BlockSpec((halo, D_MODEL), lambda i: (jnp.minimum((i + 1) * (tm // halo), n_halo_blocks - 1), 0)),
        ]
        args += [x, x]
    in_specs += [
        pl.BlockSpec((1, 6, D_MODEL), lambda i: (cond_of(i), 0, 0)),
        _resident((1, D_MODEL)),
        _resident((D_MODEL, 2 * D_FF), lead=(layer,)),
        _resident((3, D_FF)),
        _resident((1, D_FF)),
        _resident((D_FF, D_MODEL), lead=(layer,)),
    ]
    args += [mod, g, w_up, conv_w, conv_b, w_down]
    h_rows = tm + 2 * halo if span_tiles else tm
    final = final_g is not None
    if final:
        in_specs.append(_resident((1, D_MODEL)))
        args.append(final_g)
    return pl.pallas_call(
        functools.partial(_ffn_kernel, tm=tm, seq_len=seq_len, final=final),
        grid=(ntok // tm,),
        in_specs=in_specs,
        out_specs=pl.BlockSpec((tm, D_MODEL), lambda i: (i, 0)),
        out_shape=jax.ShapeDtypeStruct((ntok, D_MODEL), F32),
        scratch_shapes=[
            pltpu.VMEM((h_rows, D_MODEL), BF16),
            pltpu.VMEM((tm, D_FF), BF16),
        ],
        compiler_params=_params("parallel"),
        name="ffn",
    )(*args)


def _rope_tables(seq_len):
    nf = HEAD_DIM // 4
    n_rows = seq_len // GRID_W
    inv = ROPE_THETA ** (-jnp.arange(nf, dtype=F32) / nf)
    ang_r = jnp.arange(n_rows, dtype=F32)[:, None] * inv[None, :]
    ang_c = jnp.arange(GRID_W, dtype=F32)[:, None] * inv[None, :]
    by_row = lambda a: jnp.repeat(a, GRID_W, axis=0)
    by_col = lambda a: jnp.tile(a, (n_rows, 1))
    cos_r, sin_r, cos_c, sin_c = by_row(jnp.cos(ang_r)), by_row(jnp.sin(ang_r)), by_col(jnp.cos(ang_c)), by_col(jnp.sin(ang_c))
    cos = jnp.concatenate([cos_r, cos_r, cos_c, cos_c], axis=1)
    sin = jnp.concatenate([-sin_r, sin_r, -sin_c, sin_c], axis=1)
    reps = V7X_LANES // HEAD_DIM
    return jnp.concatenate([cos] * reps, axis=1), jnp.concatenate([sin] * reps, axis=1)


def _segment_mean_matrix(width):
    seg = np.arange(width) // HEAD_DIM
    return jnp.asarray((seg[:, None] == seg[None, :]).astype(np.float32) / HEAD_DIM, dtype=BF16)


def _dup_cache(c):
    return jnp.concatenate([c[:, :, 0], c[:, :, 0], c[:, :, 1], c[:, :, 1]], axis=-1).astype(BF16)


def _ext_cache(c):
    b, n = c.shape[:2]
    ones = jnp.ones((b, n, ATTN_ONES), c.dtype)
    zeros = jnp.zeros((b, n, V7X_LANES - HEAD_DIM - ATTN_ONES), c.dtype)
    return jnp.concatenate([c[:, :, 0], ones, zeros, c[:, :, 1], ones, zeros], axis=-1).astype(BF16)


def kernel(x_prompt, x_sample, c, cache_kb, cache_vb, cache_kc, cache_vc, state_lru, c_ctx, norm1_g, norm2_g, w_mod, b_mod, w_in, lru_conv_w, lru_conv_b, lru_wa, lru_ba, lru_wx, lru_bx, lru_lam, qnorm_g, knorm_g, sink_c, w_oa, w_ob, w_oc, w_out, w_up, ffn_conv_w, ffn_conv_b, w_down, final_g):
    batch, seq, d = x_prompt.shape
    dec_batch, dec_seq, _ = x_sample.shape
    depth = w_in.shape[0]
    assert d == D_MODEL and depth == DEPTH and dec_batch + 1 <= 8

    cond8 = jnp.zeros((8, d), F32).at[0].set(c_ctx).at[1:1 + dec_batch].set(c)
    mod_all = _mod_call(cond8, w_mod, b_mod).reshape(depth, 8, 6, d)

    w_in_b = w_in[:, :, :D_TOKEN_MIX].astype(BF16)
    w_bgate_b = w_in[:, :, D_TOKEN_MIX:].astype(BF16)
    w_oa_b, w_ob_b, w_oc_b = (w.astype(BF16) for w in (w_oa, w_ob, w_oc))
    w_out_b, w_up_b, w_down_b = (w.astype(BF16) for w in (w_out, w_up, w_down))
    w_gate = (0.5 * jnp.concatenate([lru_wa, lru_wx], axis=-1)).astype(BF16)
    half_ba, half_bx = 0.5 * lru_ba, 0.5 * lru_bx
    seg_q, seg_k = _segment_mean_matrix(D_Q), _segment_mean_matrix(D_KV)
    rope_tabs = _rope_tables(dec_seq)
    qg = jnp.tile(qnorm_g, (1, N_HEADS))
    kg = jnp.tile(knorm_g, (1, N_KV))

    def layer(l, x, mod, seq_len, h0, ctx, is_last):
        is_ctx = ctx is None
        ntok = x.shape[0]
        outs = _in_proj_call(l, x, mod, norm1_g[l][None], w_in_b, qg[l][None], kg[l][None], seg_q, seg_k,
                             lru_conv_w[l], lru_conv_b[l][None], None if is_ctx else rope_tabs, seq_len,
                             emit_kv=is_ctx)
        xc, gya, qb, kbd, vbd, qc, kcd, vcd = outs[:8]
        per_seq = lambda a: a.reshape(ntok // seq_len, seq_len, D_LRU)
        lru_args = lambda dr: (w_gate, half_ba[l, dr][None], half_bx[l, dr][None], lru_lam[l, dr][None],
                               h0[:, dr].reshape(-1, D_LRU // V7X_LANES, V7X_LANES), seq_len)
        hf, last_f = _lru_call(l, per_seq(xc), *lru_args(0), reverse=False)
        oa, last_b = _lru_call(l, per_seq(xc), *lru_args(1), reverse=True, hf=hf, gya=per_seq(gya))
        oa = oa.reshape(ntok, D_LRU)
        if is_ctx:
            ob = _attn_call(qb, kbd, vbd, seq_len, "full")
            oc = _attn_call(qc, kcd, vcd, seq_len, "full", sink=sink_c[l])
        else:
            ob = _attn_call(qb, kbd, vbd, seq_len, "full", ctx=(ctx["kb"], ctx["vb"]))
            oc = _attn_call(qc, kcd, vcd, seq_len, "window", ctx=(ctx["kc"], ctx["vc"]), sink=sink_c[l])
        x = _mix_out_call(l, oa, ob, oc, x, mod, norm1_g[l][None], w_bgate_b, w_oa_b, w_ob_b, w_oc_b, w_out_b,
                          seq_len)
        x = _ffn_call(l, x, mod, norm2_g[l][None], w_up_b, ffn_conv_w[l], ffn_conv_b[l][None], w_down_b,
                      seq_len, final_g=final_g[None] if is_last else None)
        return x, outs[8:], (last_f, last_b)

    xp = x_prompt.reshape(batch * seq, d)
    xs = x_sample.reshape(dec_batch * dec_seq, d)
    zeros_h0 = jnp.zeros((batch, 2, D_LRU), F32)
    kbs, vbs, kcs, vcs, lrus = [], [], [], [], []
    for l in range(depth):
        is_last = l == depth - 1
        xp, (kb, vb, kc, vc), (last_f, last_b) = layer(l, xp, mod_all[l, 0:1], seq, zeros_h0, None, is_last)
        kbs.append(kb)
        vbs.append(vb)
        kcs.append(kc)
        vcs.append(vc)
        lrus.append(jnp.stack([last_f.reshape(batch, D_LRU), last_b.reshape(batch, D_LRU)], axis=1))
        cached = {"kb": _dup_cache(cache_kb[:, l]), "vb": _ext_cache(cache_vb[:, l]),
                  "kc": _dup_cache(cache_kc[:, l]), "vc": _ext_cache(cache_vc[:, l])}
        xs, _, _ = layer(l, xs, mod_all[l, 1:1 + dec_batch], dec_seq, state_lru[:, l], cached, is_last)

    def stack_kv(parts):
        return jnp.stack([p.reshape(batch, seq, N_KV, HEAD_DIM) for p in parts], axis=1)

    y_prompt = xp.reshape(batch, seq, d)
    y_sample = xs.reshape(dec_batch, dec_seq, d)
    return (y_prompt, y_sample, stack_kv(kbs), stack_kv(vbs), stack_kv(kcs), stack_kv(vcs),
            jnp.stack(lrus, axis=1))
```

```python
import functools

import jax
import jax.numpy as jnp
import numpy as np
from jax import lax
from jax.experimental import pallas as pl
from jax.experimental.pallas import tpu as pltpu

D_MODEL = 1024
DEPTH = 2
GRID_W = 64
D_LRU = 1024
LRU_BLOCKS = 8
LRU_BLOCK = D_LRU // LRU_BLOCKS
LRU_C = 8.0
HEAD_DIM = 64
N_HEADS = 8
N_KV = 2
D_Q = N_HEADS * HEAD_DIM
D_KV = N_KV * HEAD_DIM
WINDOW = 128
D_FF = 2816
ROPE_THETA = 10000.0
NORM_EPS = 1e-6
NEG_INF = -1e30
ATTN_SCALE = HEAD_DIM ** -0.5
LOG2_E = 1.4426950408889634
LN_2 = 0.6931471805599453
IN_SIZES = (D_LRU, D_LRU, D_Q, D_KV, D_KV, D_Q, D_KV, D_KV, D_MODEL, D_MODEL, D_MODEL)
IN_OFF = tuple(int(v) for v in np.cumsum((0,) + IN_SIZES))
D_IN = IN_OFF[-1]
D_TOKEN_MIX = IN_OFF[8]

V7X_LANES = 128
V7X_SUBLANES_F32 = 8
V7X_VMEM_LIMIT_BYTES = 56 * 1024 * 1024

TOKEN_TILE = 512
WIDE_TOKEN_TILE = 1024
FFN_CHUNK = 256
ATTN_Q_TILE = 256
ATTN_KEY_TILE = 256
ATTN_SEQS_PER_STEP = 8
ATTN_WINDOW_QTILES = 2
ATTN_TILES_PER_STEP = 4
ATTN_LOOKAHEAD = 5
ATTN_ONES = 16
LRU_CHUNK = 512
LRU_SEQS_PER_STEP = 4
LRU_SCAN_UNROLL = 8
MOD_COL_TILE = 1536

F32 = jnp.float32
BF16 = jnp.bfloat16


def _params(*sem):
    return pltpu.CompilerParams(dimension_semantics=sem, vmem_limit_bytes=V7X_VMEM_LIMIT_BYTES)


def _resident(shape, lead=()):
    nd = len(shape)
    lead = tuple(int(i) for i in lead)
    return pl.BlockSpec((None,) * len(lead) + tuple(shape), lambda *_: lead + (0,) * nd,
                        pipeline_mode=pl.Buffered(1))


def _sigmoid(x):
    return 1.0 / (1.0 + jnp.exp(-x))


GELU_C0 = 0.7978845608028654
GELU_C1 = 0.044715 * GELU_C0


def _gelu_tanh(x):
    half = 0.5 * x
    return half + half * jnp.tanh(x * (GELU_C0 + GELU_C1 * (x * x)))


def _rms_rows(x):
    return x * lax.rsqrt(jnp.mean(x * x, axis=-1, keepdims=True) + NORM_EPS)


def _dot(a, b):
    return jnp.dot(a, b, preferred_element_type=F32)


def _mod_kernel(c_ref, w_ref, b_ref, o_ref):
    c = c_ref[...]
    s = (c * _sigmoid(c)).astype(BF16)
    o_ref[0] = _dot(s, w_ref[0].astype(BF16)) + b_ref[0]


def _mod_call(cond8, w_mod, b_mod):
    depth, d, n = w_mod.shape
    tn = MOD_COL_TILE
    return pl.pallas_call(
        _mod_kernel,
        grid=(depth, n // tn),
        in_specs=[
            pl.BlockSpec((8, d), lambda l, j: (0, 0)),
            pl.BlockSpec((1, d, tn), lambda l, j: (l, 0, j)),
            pl.BlockSpec((1, 1, tn), lambda l, j: (l, 0, j)),
        ],
        out_specs=pl.BlockSpec((1, 8, tn), lambda l, j: (l, 0, j)),
        out_shape=jax.ShapeDtypeStruct((depth, 8, n), F32),
        compiler_params=_params("parallel", "parallel"),
        name="mod",
    )(cond8, w_mod, b_mod.reshape(depth, 1, n))


def _head_rms(x, seg_ref, g):
    sq = x * x
    hi = sq.astype(BF16)
    lo = (sq - hi.astype(F32)).astype(BF16)
    ms = _dot(hi, seg_ref[...]) + _dot(lo, seg_ref[...])
    return x * lax.rsqrt(ms + NORM_EPS) * g


def _rope(x, cos, sin):
    w = x.shape[1]
    lane = lax.broadcasted_iota(jnp.int32, x.shape, 1)
    fwd = pltpu.roll(x, w - 16, 1)
    bwd = pltpu.roll(x, 16, 1)
    partner = jnp.where((lane % 32) < 16, fwd, bwd)
    return x * cos + partner * sin


def _dup_heads(x):
    lane = lax.broadcasted_iota(jnp.int32, x.shape, 1)
    swapped = pltpu.roll(x, HEAD_DIM, 1)
    low = lane < HEAD_DIM
    return jnp.concatenate([jnp.where(low, x, swapped), jnp.where(low, swapped, x)], axis=1)


def _ext_values(x):
    lane = lax.broadcasted_iota(jnp.int32, x.shape, 1)
    swapped = pltpu.roll(x, HEAD_DIM, 1)
    low = lane < HEAD_DIM
    tail = jnp.where(lane < HEAD_DIM + ATTN_ONES, 1.0, 0.0)
    return jnp.concatenate([jnp.where(low, x, tail), jnp.where(low, swapped, tail)], axis=1)


def _lru_conv(ext, cw, cb, n):
    halo = V7X_SUBLANES_F32
    taps = (pltpu.roll(ext, 2, 0), pltpu.roll(ext, 1, 0), ext, pltpu.roll(ext, n + 2 * halo - 1, 0))
    out = cb + cw[0:1] * taps[0][halo:halo + n]
    for k in range(1, 4):
        out = out + cw[k:k + 1] * taps[k][halo:halo + n]
    return out


def _in_proj_kernel(*refs, tm, seq_len, rope, emit_kv):
    span_tiles = seq_len > tm
    it = iter(refs)
    x_ref = next(it)
    prev_ref, next_ref = (next(it), next(it)) if span_tiles else (None, None)
    mod_ref, g_ref, w_ref, qg_ref, kg_ref, seg_q_ref, seg_k_ref, cw_ref, cb_ref = (next(it) for _ in range(9))
    cos_ref = sin_ref = None
    if rope:
        cos_ref, sin_ref = next(it), next(it)
    xc_ref, gya_ref, qb_ref, kbd_ref, vbd_ref, qc_ref, kcd_ref, vcd_ref = (next(it) for _ in range(8))
    if emit_kv:
        kb_ref, vb_ref, kc_ref, vc_ref = (next(it) for _ in range(4))

    mod = mod_ref[0]
    shift, scale = mod[0:1], mod[1:2]
    gain = g_ref[...] * (1.0 + scale)

    def modulated(v):
        return (_rms_rows(v) * gain + shift).astype(BF16)

    h = modulated(x_ref[...])

    def seg(lo, hi):
        return _dot(h, w_ref[:, IN_OFF[lo]:IN_OFF[hi]])

    halo = V7X_SUBLANES_F32
    qb_raw = seg(2, 3)
    kv_b = seg(3, 5)
    qkv_c = seg(5, 8)
    xa = seg(0, 1)
    if span_tiles:
        h_halo = modulated(jnp.concatenate([prev_ref[...], next_ref[...]], axis=0))
        xa_halo = _dot(h_halo, w_ref[:, IN_OFF[0]:IN_OFF[1]])
    qb = _head_rms(qb_raw, seg_q_ref, qg_ref[...])
    kb = _head_rms(kv_b[:, :D_KV], seg_k_ref, kg_ref[...])
    vb = kv_b[:, D_KV:]
    qc = qkv_c[:, :D_Q]
    kc = qkv_c[:, D_Q:D_Q + D_KV]
    vc = qkv_c[:, D_Q + D_KV:]
    gya_ref[...] = _gelu_tanh(seg(1, 2)).astype(BF16)

    cw, cb = cw_ref[...], cb_ref[...]
    if span_tiles:
        tile_in_seq = pl.program_id(0) % (seq_len // tm)
        ext = jnp.concatenate([jnp.where(tile_in_seq == 0, 0.0, xa_halo[:halo]), xa,
                               jnp.where(tile_in_seq == seq_len // tm - 1, 0.0, xa_halo[halo:])], axis=0)
        xc_ref[...] = _lru_conv(ext, cw, cb, tm)
    else:
        zero_halo = jnp.zeros((halo, D_LRU), F32)
        for b in range(tm // seq_len):
            rows = slice(b * seq_len, (b + 1) * seq_len)
            ext = jnp.concatenate([zero_halo, xa[rows], zero_halo], axis=0)
            xc_ref[rows, :] = _lru_conv(ext, cw, cb, seq_len)
    if emit_kv:
        kb_ref[...] = kb
        vb_ref[...] = vb
        kc_ref[...] = kc
        vc_ref[...] = vc
    if rope:
        cos, sin = cos_ref[...], sin_ref[...]
        cos_q = jnp.concatenate([cos] * (D_Q // V7X_LANES), axis=1)
        sin_q = jnp.concatenate([sin] * (D_Q // V7X_LANES), axis=1)
        qb = _rope(qb, cos_q, sin_q)
        qc = _rope(qc, cos_q, sin_q)
        kb = _rope(kb, cos, sin)
        kc = _rope(kc, cos, sin)
    qb_ref[...] = (qb * (ATTN_SCALE * LOG2_E)).astype(BF16)
    qc_ref[...] = (qc * (ATTN_SCALE * LOG2_E)).astype(BF16)
    kbd_ref[...] = _dup_heads(kb).astype(BF16)
    vbd_ref[...] = _ext_values(vb).astype(BF16)
    kcd_ref[...] = _dup_heads(kc).astype(BF16)
    vcd_ref[...] = _ext_values(vc).astype(BF16)


def _in_proj_call(layer, x, mod, g, w_in, qg, kg, seg_q, seg_k, conv_w, conv_b, rope_tabs, seq_len, emit_kv):
    ntok = x.shape[0]
    tm = TOKEN_TILE
    halo = V7X_SUBLANES_F32
    tiles_per_seq = max(seq_len // tm, 1)
    ncond = mod.shape[0]
    cond_of = (lambda i: i // tiles_per_seq) if ncond > 1 else (lambda i: 0)
    row = lambda w: pl.BlockSpec((tm, w), lambda i: (i, 0))
    in_specs = [row(D_MODEL)]
    args = [x]
    if seq_len > tm:
        n_halo_blocks = ntok // halo
        in_specs += [
            pl.BlockSpec((halo, D_MODEL), lambda i: (jnp.maximum(i * (tm // halo) - 1, 0), 0)),
            pl.BlockSpec((halo, D_MODEL), lambda i: (jnp.minimum((i + 1) * (tm // halo), n_halo_blocks - 1), 0)),
        ]
        args += [x, x]
    in_specs += [
        pl.BlockSpec((1, 6, D_MODEL), lambda i: (cond_of(i), 0, 0)),
        _resident((1, D_MODEL)),
        _resident((D_MODEL, D_TOKEN_MIX), lead=(layer,)),
        _resident((1, D_Q)),
        _resident((1, D_KV)),
        _resident((D_Q, D_Q)),
        _resident((D_KV, D_KV)),
        _resident((4, D_LRU)),
        _resident((1, D_LRU)),
    ]
    args += [mod, g, w_in, qg, kg, seg_q, seg_k, conv_w, conv_b]
    rope = rope_tabs is not None
    if rope:
        in_specs += [pl.BlockSpec((tm, V7X_LANES), lambda i: (i % tiles_per_seq, 0))] * 2
        args += list(rope_tabs)
    widths = [D_LRU, D_LRU, D_Q, 2 * D_KV, 2 * D_KV, D_Q, 2 * D_KV, 2 * D_KV]
    out_specs = [row(w) for w in widths]
    out_shape = [jax.ShapeDtypeStruct((ntok, w), F32 if n == 0 else BF16) for n, w in enumerate(widths)]
    if emit_kv:
        out_specs += [row(D_KV)] * 4
        out_shape += [jax.ShapeDtypeStruct((ntok, D_KV), F32)] * 4
    return pl.pallas_call(
        functools.partial(_in_proj_kernel, tm=tm, seq_len=seq_len, rope=rope, emit_kv=emit_kv),
        grid=(ntok // tm,),
        in_specs=in_specs,
        out_specs=out_specs,
        out_shape=out_shape,
        compiler_params=_params("parallel"),
        name="in_proj",
    )(*args)


def _lru_kernel(*refs, chunk, n_chunks, n_seq, reverse):
    it = iter(refs)
    xc_ref, wg_ref, ba_ref, bx_ref, lam_ref, h0_ref = (next(it) for _ in range(6))
    if reverse:
        hf_ref, gya_ref = next(it), next(it)
    out_ref, last_ref = next(it), next(it)
    a_ref, u_ref, hs_ref, state_ref = next(it), next(it), next(it), next(it)

    fold = D_LRU // V7X_LANES

    def folded_block(c):
        return pl.ds(c, chunk, stride=fold)

    j = pl.program_id(1)
    lam = lam_ref[...]
    softplus_neg_lam = jnp.maximum(-lam, 0.0) + jnp.log1p(jnp.exp(-jnp.abs(lam)))
    slope = (-0.5 * LRU_C * LOG2_E) * softplus_neg_lam

    for s in range(n_seq):
        xc = xc_ref[s]
        xc_b = xc.astype(BF16)
        half_xc = 0.5 * xc

        for n in range(LRU_BLOCKS):
            cols = slice(n * LRU_BLOCK, (n + 1) * LRU_BLOCK)
            z = _dot(xc_b[:, cols], wg_ref[n])
            t_r = jnp.tanh(z[:, :LRU_BLOCK] + ba_ref[:, cols])
            t_i = jnp.tanh(z[:, LRU_BLOCK:] + bx_ref[:, cols])
            log2_a = slope[:, cols] + slope[:, cols] * t_r
            a = jnp.exp2(log2_a)
            one_minus_a2 = jnp.tanh(log2_a * (-LN_2)) * (a * a + 1.0)
            root = jnp.where(one_minus_a2 > 0.0, one_minus_a2 * lax.rsqrt(one_minus_a2), 0.0)
            hx = half_xc[:, cols]
            a_ref[s, folded_block(n), :] = a
            u_ref[s, folded_block(n), :] = root * (hx + hx * t_i)

    @pl.when(j == 0)
    def _():
        state_ref[...] = h0_ref[...]

    steps_per_trip = LRU_SCAN_UNROLL

    def trip(gi, hs):
        gidx = (chunk // steps_per_trip - 1 - gi) if reverse else gi
        base = pl.multiple_of(gidx * (steps_per_trip * fold), steps_per_trip * fold)
        hs = list(hs)
        for k in range(steps_per_trip):
            rows = pl.ds(base + (steps_per_trip - 1 - k if reverse else k) * fold, fold)
            for s in range(n_seq):
                hs[s] = a_ref[s, rows, :] * hs[s] + u_ref[s, rows, :]
                hs_ref[s, rows, :] = hs[s]
        return tuple(hs)

    hs = lax.fori_loop(0, chunk // steps_per_trip, trip, tuple(state_ref[s] for s in range(n_seq)))
    for s in range(n_seq):
        state_ref[s] = hs[s]

    for s in range(n_seq):
        for c in range(fold):
            cols = slice(c * V7X_LANES, (c + 1) * V7X_LANES)
            scanned = hs_ref[s, folded_block(c), :]
            if reverse:
                out_ref[s, :, cols] = ((hf_ref[s, :, cols] + scanned)
                                       * gya_ref[s, :, cols].astype(F32)).astype(out_ref.dtype)
            else:
                out_ref[s, :, cols] = scanned

    @pl.when(j == n_chunks - 1)
    def _():
        for s in range(n_seq):
            last_ref[s] = hs[s]


def _lru_call(layer, xc, w_gate, ba, bx, lam, h0, seq_len, reverse, hf=None, gya=None):
    nb = xc.shape[0]
    chunk = min(LRU_CHUNK, seq_len)
    nt = seq_len // chunk
    n_seq = min(LRU_SEQS_PER_STEP, nb)
    fold = D_LRU // V7X_LANES
    pos = (lambda j: nt - 1 - j) if reverse else (lambda j: j)
    cur = lambda b, j: (b, pos(j), 0)
    per_seq = pl.BlockSpec((n_seq, fold, V7X_LANES), lambda b, j: (b, 0, 0))
    in_specs = [
        pl.BlockSpec((n_seq, chunk, D_LRU), cur),
        _resident((LRU_BLOCKS, LRU_BLOCK, 2 * LRU_BLOCK), lead=(layer, int(reverse))),
        _resident((1, D_LRU)),
        _resident((1, D_LRU)),
        _resident((1, D_LRU)),
        per_seq,
    ]
    args = [xc, w_gate, ba, bx, lam, h0]
    folded = pltpu.VMEM((n_seq, chunk * fold, V7X_LANES), F32)
    scratch = [folded, folded, folded, pltpu.VMEM((n_seq, fold, V7X_LANES), F32)]
    if reverse:
        in_specs += [pl.BlockSpec((n_seq, chunk, D_LRU), cur)] * 2
        args += [hf, gya]
    return pl.pallas_call(
        functools.partial(_lru_kernel, chunk=chunk, n_chunks=nt, n_seq=n_seq, reverse=reverse),
        grid=(nb // n_seq, nt),
        in_specs=in_specs,
        out_specs=[pl.BlockSpec((n_seq, chunk, D_LRU), cur), per_seq],
        out_shape=[jax.ShapeDtypeStruct((nb, seq_len, D_LRU), BF16 if reverse else F32),
                   jax.ShapeDtypeStruct((nb, fold, V7X_LANES), F32)],
        scratch_shapes=scratch,
        compiler_params=_params("arbitrary", "arbitrary"),
        name="lru_bwd" if reverse else "lru_fwd",
    )(*args)


def _attn_kernel(*refs, tq, seq_len, n_seq, mode, has_ctx, has_sink):
    it = iter(refs)
    q_ref, k_ref, v_ref = next(it), next(it), next(it)
    ck_ref = cv_ref = sink_ref = None
    if has_ctx:
        ck_ref, cv_ref = next(it), next(it)
    if has_sink:
        sink_ref = next(it)
    o_ref = next(it)
    m_ref, acc_ref = next(it), next(it)

    qi = pl.program_id(1)
    group = N_HEADS // N_KV
    acc_rows = HEAD_DIM + ATTN_ONES
    acc_row = lax.broadcasted_iota(jnp.int32, (acc_rows, tq), 0)
    lane = lax.broadcasted_iota(jnp.int32, (tq, V7X_LANES), 1)
    low = lane < HEAD_DIM
    key_tile = ATTN_KEY_TILE

    slots = [(s, h) for s in range(n_seq) for h in range(N_HEADS)]
    qh = []
    for slot, (s, h) in enumerate(slots):
        qp = q_ref[s * tq:(s + 1) * tq, (h // 2) * V7X_LANES:(h // 2 + 1) * V7X_LANES]
        zero = jnp.zeros_like(qp)
        qh.append(jnp.where(low, zero, qp) if h % 2 else jnp.where(low, qp, zero))
        if has_sink:
            m_ref[slot] = jnp.full((1, tq), sink_ref[h] * LOG2_E, F32)
            acc_ref[slot] = jnp.where(acc_row < HEAD_DIM, 0.0, 1.0)
        else:
            m_ref[slot] = jnp.full((1, tq), NEG_INF, F32)
            acc_ref[slot] = jnp.zeros((acc_rows, tq), F32)

    def scores(item):
        slot, key_ref, _, keys, _ = item
        g = slots[slot][1] // group
        k = key_ref[keys, g * V7X_LANES:(g + 1) * V7X_LANES]
        return lax.dot_general(k, qh[slot], (((1,), (1,)), ((), ())), preferred_element_type=F32)

    def absorb(item, st):
        slot, _, val_ref, keys, mask = item
        g = slots[slot][1] // group
        vt = val_ref[keys, g * V7X_LANES:(g + 1) * V7X_LANES].T[:acc_rows]
        if mask is not None:
            st = jnp.where(mask, st, NEG_INF)
        m_old = m_ref[slot]
        m_new = jnp.maximum(m_old, jnp.max(st, axis=0, keepdims=True))
        alpha = jnp.exp2(m_old - m_new)
        p = jnp.exp2(st - m_new)
        acc_ref[slot] = alpha * acc_ref[slot] + _dot(vt, p.astype(BF16))
        m_ref[slot] = m_new

    def run(items):
        pending = {}
        for i in range(len(items) + ATTN_LOOKAHEAD):
            if i < len(items):
                pending[i] = scores(items[i])
            j = i - ATTN_LOOKAHEAD
            if j >= 0:
                absorb(items[j], pending.pop(j))

    def tile_items(key_ref, val_ref, keys_of_seq, mask):
        return [(slot, key_ref, val_ref, keys_of_seq(s), mask) for slot, (s, _) in enumerate(slots)]

    head_items = tile_items(ck_ref, cv_ref, lambda s: slice(None), None) if has_ctx else []
    if mode == "full":
        n_tiles = seq_len // key_tile
        if n_tiles == 1:
            run(head_items + tile_items(k_ref, v_ref, lambda s: slice(s * seq_len, (s + 1) * seq_len), None))
        else:
            per_step = ATTN_TILES_PER_STEP

            def trip_items(t):
                items = []
                for u in range(per_step):
                    keys = pl.ds(pl.multiple_of((t * per_step + u) * key_tile, key_tile), key_tile)
                    items += tile_items(k_ref, v_ref, lambda s: keys, None)
                return items

            run(head_items + trip_items(0))

            def body(t, carry):
                run(trip_items(t))
                return carry
            lax.fori_loop(1, n_tiles // per_step, body, 0)
    else:
        span = tq + 2 * WINDOW
        items = head_items
        for s in range(n_seq):
            q0 = (qi * n_seq + s) * tq
            start = pl.multiple_of(jnp.clip(q0 - WINDOW, 0, seq_len - span), WINDOW)
            for u in range(span // key_tile):
                k_abs = start + u * key_tile + lax.broadcasted_iota(jnp.int32, (key_tile, tq), 0)
                q_abs = q0 + lax.broadcasted_iota(jnp.int32, (key_tile, tq), 1)
                keys = pl.ds(start + u * key_tile, key_tile)
                near = jnp.abs(q_abs - k_abs) <= WINDOW
                items = items + [(slot, k_ref, v_ref, keys, near) for slot, (ss, _) in enumerate(slots) if ss == s]
        run(items)

    def normalised(slot):
        acc = acc_ref[slot]
        return (acc[:HEAD_DIM] * (1.0 / acc[HEAD_DIM:HEAD_DIM + 1])).T

    for s in range(n_seq):
        for pair in range(N_HEADS // 2):
            halves = [normalised(s * N_HEADS + 2 * pair), normalised(s * N_HEADS + 2 * pair + 1)]
            o_ref[s * tq:(s + 1) * tq, pair * V7X_LANES:(pair + 1) * V7X_LANES] = (
                jnp.concatenate(halves, axis=1).astype(o_ref.dtype))


def _attn_call(q, kd, vd, seq_len, mode, ctx=None, sink=None):
    ntok = q.shape[0]
    nb = ntok // seq_len
    tq = min(ATTN_Q_TILE, seq_len)
    nq = seq_len // tq
    if nq == 1:
        n_seq, key_rows, grid = ATTN_SEQS_PER_STEP, ATTN_SEQS_PER_STEP * seq_len, (nb // ATTN_SEQS_PER_STEP, 1)
    else:
        n_seq = ATTN_WINDOW_QTILES if mode == "window" else 1
        key_rows, grid = seq_len, (nb, nq // n_seq)
    q_steps = grid[1]
    in_specs = [
        pl.BlockSpec((n_seq * tq, D_Q), lambda b, i: (b * q_steps + i, 0)),
        pl.BlockSpec((key_rows, 2 * D_KV), lambda b, i: (b, 0)),
        pl.BlockSpec((key_rows, 2 * D_KV), lambda b, i: (b, 0)),
    ]
    args = [q, kd, vd]
    if ctx is not None:
        past = ctx[0].shape[1]
        in_specs += [pl.BlockSpec((None, past, 2 * D_KV), lambda b, i: (b, 0, 0))] * 2
        args += list(ctx)
    if sink is not None:
        in_specs.append(pl.BlockSpec(memory_space=pltpu.SMEM))
        args.append(sink)
    return pl.pallas_call(
        functools.partial(_attn_kernel, tq=tq, seq_len=seq_len, n_seq=n_seq, mode=mode,
                          has_ctx=ctx is not None, has_sink=sink is not None),
        grid=grid,
        in_specs=in_specs,
        out_specs=pl.BlockSpec((n_seq * tq, D_Q), lambda b, i: (b * q_steps + i, 0)),
        out_shape=jax.ShapeDtypeStruct((ntok, D_Q), BF16),
        scratch_shapes=[
            pltpu.VMEM((n_seq * N_HEADS, 1, tq), F32),
            pltpu.VMEM((n_seq * N_HEADS, HEAD_DIM + ATTN_ONES, tq), F32),
        ],
        compiler_params=_params("parallel", "parallel"),
        name="attn_" + mode + ("_sink" if sink is not None else ""),
    )(*args)


def _mix_out_kernel(oa_ref, ob_ref, oc_ref, x_ref, mod_ref, g_ref, wgate_ref,
                    woa_ref, wob_ref, woc_ref, wout_ref, o_ref):
    mod = mod_ref[0]
    x = x_ref[...]
    h = (_rms_rows(x) * (g_ref[...] * (1.0 + mod[1:2])) + mod[0:1]).astype(BF16)

    def branch(k, o_ref_k, w_ref_k):
        gate = _sigmoid(_dot(h, wgate_ref[:, k * D_MODEL:(k + 1) * D_MODEL]))
        return gate * _dot(o_ref_k[...], w_ref_k[...])

    merged = branch(0, oa_ref, woa_ref) + branch(1, ob_ref, wob_ref) + branch(2, oc_ref, woc_ref)
    out = _dot(merged.astype(BF16), wout_ref[...])
    o_ref[...] = x + mod[2:3] * out


def _mix_out_call(layer, oa, ob, oc, x, mod, g, w_gate, w_oa, w_ob, w_oc, w_out, seq_len):
    ntok = x.shape[0]
    tm = WIDE_TOKEN_TILE
    tiles_per_seq = max(seq_len // tm, 1)
    cond_of = (lambda i: i // tiles_per_seq) if mod.shape[0] > 1 else (lambda i: 0)
    row = lambda w: pl.BlockSpec((tm, w), lambda i: (i, 0))
    return pl.pallas_call(
        _mix_out_kernel,
        grid=(ntok // tm,),
        in_specs=[row(D_LRU), row(D_Q), row(D_Q), row(D_MODEL),
                  pl.BlockSpec((1, 6, D_MODEL), lambda i: (cond_of(i), 0, 0)),
                  _resident((1, D_MODEL)), _resident((D_MODEL, D_IN - D_TOKEN_MIX), lead=(layer,)),
                  _resident((D_LRU, D_MODEL), lead=(layer,)), _resident((D_Q, D_MODEL), lead=(layer,)),
                  _resident((D_Q, D_MODEL), lead=(layer,)), _resident((D_MODEL, D_MODEL), lead=(layer,))],
        out_specs=row(D_MODEL),
        out_shape=jax.ShapeDtypeStruct((ntok, D_MODEL), F32),
        compiler_params=_params("parallel"),
        name="mix_out",
    )(oa, ob, oc, x, mod, g, w_gate, w_oa, w_ob, w_oc, w_out)


def _ffn_kernel(*refs, tm, seq_len, final):
    span_tiles = seq_len > tm
    it = iter(refs)
    x_ref = next(it)
    prev_ref, next_ref = (next(it), next(it)) if span_tiles else (None, None)
    mod_ref, g_ref, wup_ref, cw_ref, cb_ref, wdn_ref = (next(it) for _ in range(6))
    fg_ref = next(it) if final else None
    o_ref = next(it)
    h_ref, act_ref = next(it), next(it)

    halo = V7X_SUBLANES_F32
    sub = min(tm, seq_len)
    mod = mod_ref[0]
    shift, scale, gate2 = mod[3:4], mod[4:5], mod[5:6]
    gain = g_ref[...] * (1.0 + scale)
    x = x_ref[...]

    def modulated(v):
        return (_rms_rows(v) * gain + shift).astype(BF16)

    if span_tiles:
        tile_in_seq = pl.program_id(0) % (seq_len // tm)
        at_start, at_end = tile_in_seq == 0, tile_in_seq == seq_len // tm - 1
        h_ref[0:halo, :] = modulated(prev_ref[...])
        h_ref[halo:halo + tm, :] = modulated(x)
        h_ref[halo + tm:, :] = modulated(next_ref[...])
        centre = slice(halo, halo + tm)
    else:
        h_ref[...] = modulated(x)
        centre = slice(None)

    cw = cw_ref[...]
    zero_halo = jnp.zeros((halo, FFN_CHUNK), F32)
    for c in range(D_FF // FFN_CHUNK):
        cols = slice(c * FFN_CHUNK, (c + 1) * FFN_CHUNK)
        vcols = slice(D_FF + c * FFN_CHUNK, D_FF + (c + 1) * FFN_CHUNK)
        pre = _dot(h_ref[...], wup_ref[:, cols])
        val = _dot(h_ref[centre, :], wup_ref[:, vcols])
        parts = []
        for b in range(tm // sub):
            if span_tiles:
                ext = jnp.concatenate([jnp.where(at_start, 0.0, pre[:halo]), pre[halo:halo + tm],
                                       jnp.where(at_end, 0.0, pre[halo + tm:])], axis=0)
            else:
                ext = jnp.concatenate([zero_halo, pre[b * sub:(b + 1) * sub], zero_halo], axis=0)
            before = pltpu.roll(ext, 1, 0)[halo:halo + sub]
            after = pltpu.roll(ext, sub + 2 * halo - 1, 0)[halo:halo + sub]
            parts.append(cb_ref[:, cols] + cw[0:1, cols] * before + cw[1:2, cols] * ext[halo:halo + sub]
                         + cw[2:3, cols] * after)
        gate = parts[0] if len(parts) == 1 else jnp.concatenate(parts, axis=0)
        act_ref[:, cols] = (_gelu_tanh(gate) * val).astype(BF16)

    y = x + gate2 * _dot(act_ref[...], wdn_ref[...])
    if final:
        y = _rms_rows(y) * fg_ref[...]
    o_ref[...] = y


def _ffn_call(layer, x, mod, g, w_up, conv_w, conv_b, w_down, seq_len, final_g=None):
    ntok = x.shape[0]
    tm = WIDE_TOKEN_TILE
    halo = V7X_SUBLANES_F32
    tiles_per_seq = max(seq_len // tm, 1)
    cond_of = (lambda i: i // tiles_per_seq) if mod.shape[0] > 1 else (lambda i: 0)
    n_halo_blocks = ntok // halo
    span_tiles = seq_len > tm
    in_specs = [pl.BlockSpec((tm, D_MODEL), lambda i: (i, 0))]
    args = [x]
    if span_tiles:
        in_specs += [
            pl.BlockSpec((halo, D_MODEL), lambda i: (jnp.maximum(i * (tm // halo) - 1, 0), 0)),
            pl.BlockSpec((halo, D_MODEL), lambda i: (jnp.minimum((i + 1) * (tm // halo), n_halo_blocks - 1), 0)),
        ]
        args += [x, x]
    in_specs += [
        pl.BlockSpec((1, 6, D_MODEL), lambda i: (cond_of(i), 0, 0)),
        _resident((1, D_MODEL)),
        _resident((D_MODEL, 2 * D_FF), lead=(layer,)),
        _resident((3, D_FF)),
        _resident((1, D_FF)),
        _resident((D_FF, D_MODEL), lead=(layer,)),
    ]
    args += [mod, g, w_up, conv_w, conv_b, w_down]
    h_rows = tm + 2 * halo if span_tiles else tm
    final = final_g is not None
    if final:
        in_specs.append(_resident((1, D_MODEL)))
        args.append(final_g)
    return pl.pallas_call(
        functools.partial(_ffn_kernel, tm=tm, seq_len=seq_len, final=final),
        grid=(ntok // tm,),
        in_specs=in_specs,
        out_specs=pl.BlockSpec((tm, D_MODEL), lambda i: (i, 0)),
        out_shape=jax.ShapeDtypeStruct((ntok, D_MODEL), F32),
        scratch_shapes=[
            pltpu.VMEM((h_rows, D_MODEL), BF16),
            pltpu.VMEM((tm, D_FF), BF16),
        ],
        compiler_params=_params("parallel"),
        name="ffn",
    )(*args)


def _rope_tables(seq_len):
    nf = HEAD_DIM // 4
    n_rows = seq_len // GRID_W
    inv = ROPE_THETA ** (-jnp.arange(nf, dtype=F32) / nf)
    ang_r = jnp.arange(n_rows, dtype=F32)[:, None] * inv[None, :]
    ang_c = jnp.arange(GRID_W, dtype=F32)[:, None] * inv[None, :]
    by_row = lambda a: jnp.repeat(a, GRID_W, axis=0)
    by_col = lambda a: jnp.tile(a, (n_rows, 1))
    cos_r, sin_r, cos_c, sin_c = by_row(jnp.cos(ang_r)), by_row(jnp.sin(ang_r)), by_col(jnp.cos(ang_c)), by_col(jnp.sin(ang_c))
    cos = jnp.concatenate([cos_r, cos_r, cos_c, cos_c], axis=1)
    sin = jnp.concatenate([-sin_r, sin_r, -sin_c, sin_c], axis=1)
    reps = V7X_LANES // HEAD_DIM
    return jnp.concatenate([cos] * reps, axis=1), jnp.concatenate([sin] * reps, axis=1)


def _segment_mean_matrix(width):
    seg = np.arange(width) // HEAD_DIM
    return jnp.asarray((seg[:, None] == seg[None, :]).astype(np.float32) / HEAD_DIM, dtype=BF16)


def _dup_cache(c):
    return jnp.concatenate([c[:, :, 0], c[:, :, 0], c[:, :, 1], c[:, :, 1]], axis=-1).astype(BF16)


def _ext_cache(c):
    b, n = c.shape[:2]
    ones = jnp.ones((b, n, ATTN_ONES), c.dtype)
    zeros = jnp.zeros((b, n, V7X_LANES - HEAD_DIM - ATTN_ONES), c.dtype)
    return jnp.concatenate([c[:, :, 0], ones, zeros, c[:, :, 1], ones, zeros], axis=-1).astype(BF16)


def kernel(x_prompt, x_sample, c, cache_kb, cache_vb, cache_kc, cache_vc, state_lru, c_ctx, norm1_g, norm2_g, w_mod, b_mod, w_in, lru_conv_w, lru_conv_b, lru_wa, lru_ba, lru_wx, lru_bx, lru_lam, qnorm_g, knorm_g, sink_c, w_oa, w_ob, w_oc, w_out, w_up, ffn_conv_w, ffn_conv_b, w_down, final_g):
    batch, seq, d = x_prompt.shape
    dec_batch, dec_seq, _ = x_sample.shape
    depth = w_in.shape[0]
    assert d == D_MODEL and depth == DEPTH and dec_batch + 1 <= 8

    cond8 = jnp.zeros((8, d), F32).at[0].set(c_ctx).at[1:1 + dec_batch].set(c)
    mod_all = _mod_call(cond8, w_mod, b_mod).reshape(depth, 8, 6, d)

    w_in_b = w_in[:, :, :D_TOKEN_MIX].astype(BF16)
    w_bgate_b = w_in[:, :, D_TOKEN_MIX:].astype(BF16)
    w_oa_b, w_ob_b, w_oc_b = (w.astype(BF16) for w in (w_oa, w_ob, w_oc))
    w_out_b, w_up_b, w_down_b = (w.astype(BF16) for w in (w_out, w_up, w_down))
    w_gate = (0.5 * jnp.concatenate([lru_wa, lru_wx], axis=-1)).astype(BF16)
    half_ba, half_bx = 0.5 * lru_ba, 0.5 * lru_bx
    seg_q, seg_k = _segment_mean_matrix(D_Q), _segment_mean_matrix(D_KV)
    rope_tabs = _rope_tables(dec_seq)
    qg = jnp.tile(qnorm_g, (1, N_HEADS))
    kg = jnp.tile(knorm_g, (1, N_KV))

    def layer(l, x, mod, seq_len, h0, ctx, is_last):
        is_ctx = ctx is None
        ntok = x.shape[0]
        outs = _in_proj_call(l, x, mod, norm1_g[l][None], w_in_b, qg[l][None], kg[l][None], seg_q, seg_k,
                             lru_conv_w[l], lru_conv_b[l][None], None if is_ctx else rope_tabs, seq_len,
                             emit_kv=is_ctx)
        xc, gya, qb, kbd, vbd, qc, kcd, vcd = outs[:8]
        per_seq = lambda a: a.reshape(ntok // seq_len, seq_len, D_LRU)
        lru_args = lambda dr: (w_gate, half_ba[l, dr][None], half_bx[l, dr][None], lru_lam[l, dr][None],
                               h0[:, dr].reshape(-1, D_LRU // V7X_LANES, V7X_LANES), seq_len)
        hf, last_f = _lru_call(l, per_seq(xc), *lru_args(0), reverse=False)
        oa, last_b = _lru_call(l, per_seq(xc), *lru_args(1), reverse=True, hf=hf, gya=per_seq(gya))
        oa = oa.reshape(ntok, D_LRU)
        if is_ctx:
            ob = _attn_call(qb, kbd, vbd, seq_len, "full")
            oc = _attn_call(qc, kcd, vcd, seq_len, "full", sink=sink_c[l])
        else:
            ob = _attn_call(qb, kbd, vbd, seq_len, "full", ctx=(ctx["kb"], ctx["vb"]))
            oc = _attn_call(qc, kcd, vcd, seq_len, "window", ctx=(ctx["kc"], ctx["vc"]), sink=sink_c[l])
        x = _mix_out_call(l, oa, ob, oc, x, mod, norm1_g[l][None], w_bgate_b, w_oa_b, w_ob_b, w_oc_b, w_out_b,
                          seq_len)
        x = _ffn_call(l, x, mod, norm2_g[l][None], w_up_b, ffn_conv_w[l], ffn_conv_b[l][None], w_down_b,
                      seq_len, final_g=final_g[None] if is_last else None)
        return x, outs[8:], (last_f, last_b)

    xp = x_prompt.reshape(batch * seq, d)
    xs = x_sample.reshape(dec_batch * dec_seq, d)
    zeros_h0 = jnp.zeros((batch, 2, D_LRU), F32)
    kbs, vbs, kcs, vcs, lrus = [], [], [], [], []
    for l in range(depth):
        is_last = l == depth - 1
        xp, (kb, vb, kc, vc), (last_f, last_b) = layer(l, xp, mod_all[l, 0:1], seq, zeros_h0, None, is_last)
        kbs.append(kb)
        vbs.append(vb)
        kcs.append(kc)
        vcs.append(vc)
        lrus.append(jnp.stack([last_f.reshape(batch, D_LRU), last_b.reshape(batch, D_LRU)], axis=1))
        cached = {"kb": _dup_cache(cache_kb[:, l]), "vb": _ext_cache(cache_vb[:, l]),
                  "kc": _dup_cache(cache_kc[:, l]), "vc": _ext_cache(cache_vc[:, l])}
        xs, _, _ = layer(l, xs, mod_all[l, 1:1 + dec_batch], dec_seq, state_lru[:, l], cached, is_last)

    def stack_kv(parts):
        return jnp.stack([p.reshape(batch, seq, N_KV, HEAD_DIM) for p in parts], axis=1)

    y_prompt = xp.reshape(batch, seq, d)
    y_sample = xs.reshape(dec_batch, dec_seq, d)
    return (y_prompt, y_sample, stack_kv(kbs), stack_kv(vbs), stack_kv(kcs), stack_kv(vcs),
            jnp.stack(lrus, axis=1))
```
